```python
import math
import jax
import jax.numpy as jnp
from jax import lax
import numpy as np

D_MODEL = 1024
BATCH = 4
SEQ = 8192
DEPTH = 2

CHUNK = 64
N_META = 16
N_PAD = CHUNK - N_META
CONV_K = 4
NORM_EPS = 1e-6

H_A = 6
DH_A = 128
D_A = H_A * DH_A
H_B = 8
DH_B = 64
D_B = H_B * DH_B
R_W = 64
R_A = 64
GN_EPS_B = 64e-5
H_C = 12
DH_C = 64
D_C = H_C * DH_C
G_C = 4
N_C = 128

D_MIX = D_A + D_B + D_C
P_A = 4 * D_A + 2 * H_A
P_B = 4 * D_B + R_W + R_A
P_C = 2 * D_C + 2 * G_C * N_C + H_C
D_PROJ = P_A + P_B + P_C

kernel_name = "hybrid_deltanet_rwkv7_mamba2_trunk"


def _splits(sizes):
    return [int(s) for s in np.cumsum(sizes)[:-1]]


def rms_norm(x, w, eps=NORM_EPS):
    xf = x.astype(jnp.float32)
    y = xf * lax.rsqrt(jnp.mean(xf * xf, axis=-1, keepdims=True) + eps)
    return (y * w.astype(jnp.float32)).astype(x.dtype)


def l2_normalize(x, eps=1e-6):
    xf = x.astype(jnp.float32)
    return xf * lax.rsqrt(jnp.sum(xf * xf, axis=-1, keepdims=True) + eps)


def causal_dwconv(u, w):
    c = u.shape[-1]
    return lax.conv_general_dilated(
        u, w[:, None, :].astype(u.dtype), window_strides=(1,), padding=[(CONV_K - 1, 0)],
        dimension_numbers=("NWC", "WIO", "NWC"), feature_group_count=c)


def pad_front(u, n):
    return jnp.pad(u, [(0, 0), (n, 0)] + [(0, 0)] * (u.ndim - 2))


def token_shift(u):
    return jnp.pad(u, ((0, 0), (1, 0), (0, 0)))[:, :-1]


def gated_delta_rule(q, k, v, g, beta):
    bsz, t_len, n_h, dk = q.shape
    dv = v.shape[-1]
    n_c = t_len // CHUNK

    def to_chunks(u):
        u = u.astype(jnp.float32).reshape((bsz, n_c, CHUNK) + u.shape[2:])
        return jnp.moveaxis(u, 3, 1)

    q, k, v, g, beta = map(to_chunks, (q, k, v, g, beta))
    q = q * (dk ** -0.5)
    gc = jnp.cumsum(g, axis=-1)
    incl = jnp.tril(jnp.ones((CHUNK, CHUNK), dtype=bool))
    strict = jnp.tril(jnp.ones((CHUNK, CHUNK), dtype=bool), -1)
    seg = gc[..., :, None] - gc[..., None, :]
    dmask = jnp.where(incl, jnp.exp(jnp.where(incl, seg, 0.0)), 0.0)
    kb = k * beta[..., None]
    a_mat = jnp.where(strict, jnp.einsum("bhnid,bhnjd->bhnij", kb, k) * dmask, 0.0)
    rhs = jnp.concatenate([v * beta[..., None], kb * jnp.exp(gc)[..., None]], axis=-1)
    sol = lax.linalg.triangular_solve(a_mat + jnp.eye(CHUNK, dtype=jnp.float32), rhs,
                                      left_side=True, lower=True, unit_diagonal=True)
    u_c, w_c = sol[..., :dv], sol[..., dv:]
    attn = jnp.einsum("bhnid,bhnjd->bhnij", q, k) * dmask
    q_dec = q * jnp.exp(gc)[..., None]
    k_dec = k * jnp.exp(gc[..., -1:] - gc)[..., None]
    g_tot = jnp.exp(gc[..., -1])

    def step(state, inp):
        u_i, w_i, q_i, k_i, a_i, g_i = inp
        v_new = u_i - jnp.einsum("bhck,bhkv->bhcv", w_i, state)
        o_i = jnp.einsum("bhck,bhkv->bhcv", q_i, state) + jnp.einsum("bhij,bhjv->bhiv", a_i, v_new)
        state = state * g_i[..., None, None] + jnp.einsum("bhck,bhcv->bhkv", k_i, v_new)
        return state, o_i

    xs = tuple(jnp.moveaxis(u, 2, 0) for u in (u_c, w_c, q_dec, k_dec, attn, g_tot))
    s0 = jnp.zeros((bsz, n_h, dk, dv), jnp.float32)
    _, o = lax.scan(step, s0, xs)
    return jnp.transpose(o, (1, 0, 3, 2, 4)).reshape(bsz, t_len, n_h, dv)


def rwkv7_recurrence(r, w, k, v, a, b):
    bsz, _, n_h, dh = r.shape

    def step(state, inp):
        r_t, w_t, k_t, v_t, a_t, b_t = inp
        sa = jnp.einsum("bhvk,bhk->bhv", state, a_t)
        state = (state * w_t[:, :, None, :] + sa[..., None] * b_t[:, :, None, :]
                 + v_t[..., None] * k_t[:, :, None, :])
        return state, jnp.einsum("bhvk,bhk->bhv", state, r_t)

    xs = tuple(jnp.moveaxis(u.astype(jnp.float32), 1, 0) for u in (r, w, k, v, a, b))
    s0 = jnp.zeros((bsz, n_h, dh, dh), jnp.float32)
    _, y = lax.scan(step, s0, xs)
    return jnp.moveaxis(y, 0, 1)


def _segsum(a):
    t = a.shape[-1]
    cs = jnp.cumsum(a, axis=-1)
    mask = jnp.tril(jnp.ones((t, t), dtype=bool))
    return jnp.where(mask, cs[..., :, None] - cs[..., None, :], -jnp.inf)


def ssd_chunked(x, a, b_mat, c_mat):
    bsz, t_len, n_h, hp = x.shape
    n_c = t_len // CHUNK
    x, b_mat, c_mat = [u.reshape((bsz, n_c, CHUNK) + u.shape[2:]) for u in (x, b_mat, c_mat)]
    a = jnp.moveaxis(a.reshape(bsz, n_c, CHUNK, n_h), 3, 1)
    a_cs = jnp.cumsum(a, axis=-1)
    l_mat = jnp.exp(_segsum(a))
    y_diag = jnp.einsum("bclhn,bcshn,bhcls,bcshp->bclhp", c_mat, b_mat, l_mat, x)
    decay_states = jnp.exp(a_cs[..., -1:] - a_cs)
    states = jnp.einsum("bclhn,bhcl,bclhp->bchpn", b_mat, decay_states, x)
    states = jnp.concatenate([jnp.zeros_like(states[:, :1]), states], axis=1)
    decay_chunk = jnp.exp(_segsum(jnp.pad(a_cs[..., -1], ((0, 0), (0, 0), (1, 0)))))
    states = jnp.einsum("bhzc,bchpn->bzhpn", decay_chunk, states)[:, :-1]
    y_off = jnp.einsum("bclhn,bchpn,bhcl->bclhp", c_mat, states, jnp.exp(a_cs))
    return (y_diag + y_off).reshape(bsz, t_len, n_h, hp)


def gated_deltanet_group(p, conv_w, a_log, dt_bias, norm_w):
    out_dtype = p.dtype
    p = p.astype(jnp.float32)
    bsz, seq_len, _ = p.shape
    qkv, z, a_dt, b_raw = jnp.split(p, _splits([3 * D_A, D_A, H_A, H_A]), axis=-1)
    qkv = jax.nn.silu(causal_dwconv(qkv, conv_w.astype(jnp.float32)))
    q, k, v = [u.reshape(bsz, seq_len, H_A, DH_A) for u in jnp.split(qkv, 3, axis=-1)]
    g = -jnp.exp(a_log.astype(jnp.float32)) * jax.nn.softplus(a_dt + dt_bias.astype(jnp.float32))
    beta = jax.nn.sigmoid(b_raw)
    o = gated_delta_rule(*(pad_front(u, N_PAD) for u in (l2_normalize(q), l2_normalize(k), v, g, beta)))
    o = o[:, N_PAD:]
    o = rms_norm(o, norm_w) * jax.nn.silu(z.reshape(bsz, seq_len, H_A, DH_A))
    return o.reshape(bsz, seq_len, D_A).astype(out_dtype)


def rwkv7_group(p, mu, w0, w2, a0, a2, k_k, k_a, r_k, ln_w, ln_b):
    out_dtype = p.dtype
    p = p.astype(jnp.float32)
    bsz, seq_len, _ = p.shape
    p = p + (token_shift(p) - p) * mu.astype(jnp.float32)
    r, k, v, gate, w_lo, a_lo = jnp.split(p, _splits([D_B] * 4 + [R_W, R_A]), axis=-1)
    log_w = -jnp.exp(-jax.nn.softplus(-(w0 + jnp.tanh(w_lo) @ w2.astype(jnp.float32))) - 0.5)
    a = jax.nn.sigmoid(a0 + a_lo @ a2.astype(jnp.float32))
    heads = lambda u: u.reshape(bsz, seq_len, H_B, DH_B)
    kk = l2_normalize(heads(k * k_k))
    k = k * (1.0 + (a - 1.0) * k_a)
    r, k, v, a, log_w = map(heads, (r, k, v, a, log_w))
    y = rwkv7_recurrence(r, jnp.exp(log_w), k, v, -kk, kk * a)
    mean = jnp.mean(y, axis=-1, keepdims=True)
    var = jnp.mean(jnp.square(y - mean), axis=-1, keepdims=True)
    y = ((y - mean) * lax.rsqrt(var + GN_EPS_B)).reshape(bsz, seq_len, D_B) * ln_w + ln_b
    y = y + (jnp.sum(r * k * r_k, axis=-1, keepdims=True) * v).reshape(bsz, seq_len, D_B)
    return (y * jax.nn.silu(gate)).astype(out_dtype)


def mamba2_group(p, conv_w, conv_b, dt_bias, a_log, d_skip, norm_w):
    out_dtype = p.dtype
    p = p.astype(jnp.float32)
    bsz, seq_len, _ = p.shape
    z, xbc, dt = jnp.split(p, _splits([D_C, D_C + 2 * G_C * N_C, H_C]), axis=-1)
    xbc = jax.nn.silu(causal_dwconv(xbc, conv_w.astype(jnp.float32)) + conv_b)
    xs, b_mat, c_mat = jnp.split(xbc, _splits([D_C, G_C * N_C, G_C * N_C]), axis=-1)
    xs = xs.reshape(bsz, seq_len, H_C, DH_C)
    to_heads = lambda u: jnp.repeat(u.reshape(bsz, seq_len, G_C, N_C), H_C // G_C, axis=2)
    dt = jax.nn.softplus(dt + dt_bias.astype(jnp.float32))
    a = -jnp.exp(a_log.astype(jnp.float32))
    y = ssd_chunked(*(pad_front(u, N_PAD) for u in (xs * dt[..., None], a * dt, to_heads(b_mat), to_heads(c_mat))))
    y = (y[:, N_PAD:] + d_skip[:, None] * xs).reshape(bsz, seq_len, D_C)
    y = y * jax.nn.silu(z)
    yg = y.reshape(bsz, seq_len, G_C, D_C // G_C)
    yg = yg * lax.rsqrt(jnp.mean(yg * yg, axis=-1, keepdims=True) + NORM_EPS)
    return (yg.reshape(bsz, seq_len, D_C) * norm_w).astype(out_dtype)


def setup_inputs(seed: int = 0) -> dict:
    key = jax.random.key(seed)
    ks = jax.random.split(key, 26)
    f32 = jnp.float32

    def nrm(k, shape, scale):
        return scale * jax.random.normal(k, shape, f32)

    def gain(k, shape):
        return 1.0 + 0.05 * jax.random.normal(k, shape, f32)

    def a_log_init(k, shape):
        return jnp.log(jax.random.uniform(k, shape, f32, minval=1.0, maxval=16.0))

    def dt_bias_init(k, shape):
        dt = jnp.exp(jax.random.uniform(k, shape, f32, minval=math.log(1e-3), maxval=math.log(1e-1)))
        return dt + jnp.log(-jnp.expm1(-dt))

    conv_c = D_C + 2 * G_C * N_C
    return {
        "x": nrm(ks[0], (BATCH, SEQ, D_MODEL), 1.0),
        "meta_tokens": nrm(ks[1], (N_META, D_MODEL), 1.0),
        "norm_pre": gain(ks[2], (DEPTH, D_MODEL)),
        "norm_post": gain(ks[3], (DEPTH, D_MODEL)),
        "w_in": nrm(ks[4], (DEPTH, D_MODEL, D_PROJ), D_MODEL ** -0.5),
        "w_out": nrm(ks[5], (DEPTH, D_MIX, D_MODEL), D_MIX ** -0.5),
        "dn_conv": nrm(ks[6], (DEPTH, CONV_K, 3 * D_A), CONV_K ** -0.5),
        "dn_A_log": a_log_init(ks[7], (DEPTH, H_A)),
        "dn_dt_bias": dt_bias_init(ks[8], (DEPTH, H_A)),
        "dn_norm": gain(ks[9], (DEPTH, DH_A)),
        "rw_mu": jax.random.uniform(ks[10], (DEPTH, P_B), f32),
        "rw_w0": jax.random.uniform(ks[11], (DEPTH, D_B), f32, minval=-2.0, maxval=2.0),
        "rw_w2": nrm(ks[12], (DEPTH, R_W, D_B), 0.5 * R_W ** -0.5),
        "rw_a0": nrm(ks[13], (DEPTH, D_B), 0.1),
        "rw_a2": nrm(ks[14], (DEPTH, R_A, D_B), 0.5 * R_A ** -0.5),
        "rw_k_k": 0.85 + 0.05 * jax.random.normal(ks[15], (DEPTH, D_B), f32),
        "rw_k_a": gain(ks[16], (DEPTH, D_B)),
        "rw_r_k": nrm(ks[17], (DEPTH, H_B, DH_B), 0.1),
        "rw_ln_w": gain(ks[18], (DEPTH, D_B)),
        "rw_ln_b": nrm(ks[19], (DEPTH, D_B), 0.01),
        "mb_conv": nrm(ks[20], (DEPTH, CONV_K, conv_c), CONV_K ** -0.5),
        "mb_conv_b": nrm(ks[21], (DEPTH, conv_c), 0.01),
        "mb_dt_bias": dt_bias_init(ks[22], (DEPTH, H_C)),
        "mb_A_log": a_log_init(ks[23], (DEPTH, H_C)),
        "mb_D": gain(ks[24], (DEPTH, H_C)),
        "mb_norm": gain(ks[25], (DEPTH, D_C)),
    }


def reference(x, meta_tokens, norm_pre, norm_post, w_in, w_out,
              dn_conv, dn_A_log, dn_dt_bias, dn_norm,
              rw_mu, rw_w0, rw_w2, rw_a0, rw_a2, rw_k_k, rw_k_a, rw_r_k, rw_ln_w, rw_ln_b,
              mb_conv, mb_conv_b, mb_dt_bias, mb_A_log, mb_D, mb_norm):
    bsz = x.shape[0]
    meta = jnp.broadcast_to(meta_tokens.astype(x.dtype)[None], (bsz, N_META, D_MODEL))
    h = jnp.concatenate([meta, x], axis=1)
    for l in range(DEPTH):
        hn = rms_norm(h, norm_pre[l])
        proj = jnp.einsum("bld,dp->blp", hn, w_in[l])
        p_a, p_b, p_c = jnp.split(proj, _splits([P_A, P_B, P_C]), axis=-1)
        mixed = jnp.concatenate([
            gated_deltanet_group(p_a, dn_conv[l], dn_A_log[l], dn_dt_bias[l], dn_norm[l]),
            rwkv7_group(p_b, rw_mu[l], rw_w0[l], rw_w2[l], rw_a0[l], rw_a2[l],
                        rw_k_k[l], rw_k_a[l], rw_r_k[l], rw_ln_w[l], rw_ln_b[l]),
            mamba2_group(p_c, mb_conv[l], mb_conv_b[l], mb_dt_bias[l], mb_A_log[l], mb_D[l], mb_norm[l]),
        ], axis=-1)
        out = jnp.einsum("blm,md->bld", mixed, w_out[l])
        h = h + rms_norm(out, norm_post[l])
    return h[:, N_META:]
```

```python
import functools

import numpy as np
import jax
import jax.numpy as jnp
from jax import lax
from jax.experimental import pallas as pl
from jax.experimental.pallas import tpu as pltpu

F32 = jnp.float32
BF16 = jnp.bfloat16

D_MODEL = 1024
CHUNK = 64
N_META = 16
CONV_K = 4
NORM_EPS = 1e-6
LANES = 128
CARRY = 8

H_A, DH_A = 6, 128
D_A = H_A * DH_A
H_B, DH_B = 8, 64
D_B = H_B * DH_B
R_W = R_A = 64
GN_EPS_B = 64e-5
H_C, DH_C = 12, 64
D_C = H_C * DH_C
G_C, N_C = 4, 128
HPG = H_C // G_C
GW = 256
D_CP = G_C * GW
P_A = 4 * D_A + 2 * H_A
P_B = 4 * D_B + R_W + R_A
P_C = 2 * D_C + 2 * G_C * N_C + H_C

WA = 4 * D_A + 2 * LANES
WB = 4 * D_B + 2 * LANES
WC_CONV = D_CP + 2 * G_C * N_C
WC = D_CP + WC_CONV + LANES

VMEM_LIMIT = 56 * 1024 * 1024
MAX_TILE_CHUNKS = 5


def _dot(a, b):
    return jnp.dot(a.astype(BF16), b.astype(BF16), preferred_element_type=F32)


def _dot_nt(a, b):
    return lax.dot_general(a.astype(BF16), b.astype(BF16), (((1,), (1,)), ((), ())),
                           preferred_element_type=F32)


def _dot_tn(a, b):
    return lax.dot_general(a.astype(BF16), b.astype(BF16), (((0,), (0,)), ((), ())),
                           preferred_element_type=F32)


def _split2(x):
    hi = x.astype(BF16)
    lo = (x - hi.astype(F32)).astype(BF16)
    return hi, lo


def _split3(x):
    hi = x.astype(BF16)
    r = x - hi.astype(F32)
    mid = r.astype(BF16)
    lo = (r - mid.astype(F32)).astype(BF16)
    return hi, mid, lo


def _dot_lx3(l_bf16, x):
    hi, mid, lo = _split3(x)
    d = lambda p: jnp.dot(l_bf16, p, preferred_element_type=F32)
    return d(hi) + d(mid) + d(lo)


def _dot_x2r(x, r_bf16):
    hi, lo = _split2(x)
    d = lambda p: jnp.dot(p, r_bf16, preferred_element_type=F32)
    return d(hi) + d(lo)


def _dot_x3r(x, r_bf16):
    hi, mid, lo = _split3(x)
    d = lambda p: jnp.dot(p, r_bf16, preferred_element_type=F32)
    return d(hi) + d(mid) + d(lo)


def _dot3(a, b):
    ah, al = _split2(a)
    bh, bl = _split2(b)
    d = lambda p, q: jnp.dot(p, q, preferred_element_type=F32)
    return d(ah, bh) + d(ah, bl) + d(al, bh)


def _neumann_inverse(x, eye):
    p = eye + x
    xk = x
    for _ in range(5):
        xk = _dot3(xk, xk)
        p = p + _dot3(xk, p)
    return p


def _silu(x):
    return x * jax.nn.sigmoid(x)


def _softplus(x):
    return jnp.maximum(x, 0.0) + jnp.log1p(jnp.exp(-jnp.abs(x)))


def _iota2(shape, dim):
    return lax.broadcasted_iota(jnp.int32, shape, dim)


def _tri_masks():
    r = _iota2((CHUNK, CHUNK), 0)
    c = _iota2((CHUNK, CHUNK), 1)
    incl = r >= c
    strict = r > c
    tri = jnp.where(incl, 1.0, 0.0).astype(BF16)
    eye = jnp.where(r == c, 1.0, 0.0).astype(F32)
    return incl, strict, tri, eye


def _normed_input(h_ref, npre_ref, t_idx, rows, pad_rows):
    h = h_ref[0]
    ms = jnp.mean(h * h, axis=-1, keepdims=True)
    hn = h * lax.rsqrt(ms + NORM_EPS) * npre_ref[...]
    gid = t_idx * rows + _iota2((rows, 1), 0)
    valid = gid >= pad_rows
    hn = jnp.where(valid, hn, 0.0)
    return hn.astype(BF16), valid


def _causal_conv_block(ubuf, cw_ref, lo, rows):
    acc = None
    for j in range(CONV_K):
        start = CARRY - (CONV_K - 1) + j
        term = ubuf[start:start + rows, lo:lo + LANES] * cw_ref[j:j + 1, lo:lo + LANES]
        acc = term if acc is None else acc + term
    return acc


def _deltanet_kernel(nc, pad_rows, h_ref, npre_ref, w_ref, cw_ref, hp_ref, nw_ref, o_ref,
                     ubuf, q_s, k_s, v_s, z_s, gb_s, st_ref):
    rows = nc * CHUNK
    t_idx = pl.program_id(1)

    @pl.when(t_idx == 0)
    def _():
        ubuf[0:CARRY, :] = jnp.zeros((CARRY, 3 * D_A), F32)
        st_ref[...] = jnp.zeros(st_ref.shape, F32)

    hn, valid = _normed_input(h_ref, npre_ref, t_idx, rows, pad_rows)
    ubuf[CARRY:CARRY + rows, :] = jnp.dot(hn, w_ref[:, 0:3 * D_A], preferred_element_type=F32)
    z_s[...] = jnp.dot(hn, w_ref[:, 3 * D_A:4 * D_A], preferred_element_type=F32)
    ab = jnp.dot(hn, w_ref[:, 4 * D_A:WA], preferred_element_type=F32)

    a_log = hp_ref[0:1, :]
    dt_bias = hp_ref[1:2, :]
    g = -jnp.exp(a_log) * _softplus(ab[:, 0:LANES] + dt_bias)
    beta = jax.nn.sigmoid(ab[:, LANES:2 * LANES])
    gb_s[:, 0:LANES] = jnp.where(valid, g, 0.0)
    gb_s[:, LANES:2 * LANES] = jnp.where(valid, beta, 0.0)

    for blk in range(3 * H_A):
        lo = blk * LANES
        y = _silu(_causal_conv_block(ubuf, cw_ref, lo, rows))
        if blk < 2 * H_A:
            y = y * lax.rsqrt(jnp.sum(y * y, axis=-1, keepdims=True) + 1e-6)
        dst = (q_s, k_s, v_s)[blk // H_A]
        hl = (blk % H_A) * LANES
        dst[:, hl:hl + LANES] = y
    ubuf[0:CARRY, :] = ubuf[rows:rows + CARRY, :]

    incl, strict, tri, eye = _tri_masks()
    scale = DH_A ** -0.5
    nw = nw_ref[...]

    def chunk_body(c, carry):
        r0 = pl.multiple_of(c * CHUNK, CHUNK)
        rs = pl.ds(r0, CHUNK)
        gc = _dot_lx3(tri, gb_s[rs, 0:LANES])
        gct = gc.T
        beta_c = gb_s[rs, LANES:2 * LANES]
        for hd in range(H_A):
            hs = slice(hd * DH_A, (hd + 1) * DH_A)
            q = q_s[rs, hs]
            k = k_s[rs, hs]
            v = v_s[rs, hs]
            gcol = gc[:, hd:hd + 1]
            grow = gct[hd:hd + 1, :]
            bcol = beta_c[:, hd:hd + 1]
            dmask = jnp.where(incl, jnp.exp(jnp.where(incl, gcol - grow, 0.0)), 0.0)
            kb = k * bcol
            a_mat = jnp.where(strict, _dot_nt(kb, k) * dmask, 0.0)
            t_inv = _neumann_inverse(-a_mat, eye)
            eg = jnp.exp(gcol)
            rhs = jnp.concatenate([v * bcol, kb * eg], axis=1)
            sol = _dot3(t_inv, rhs)
            u_c = sol[:, 0:DH_A]
            w_c = sol[:, DH_A:2 * DH_A]
            qs = q * scale
            attn = _dot_nt(qs, k) * dmask
            g_last = gc[CHUNK - 1:CHUNK, hd:hd + 1]
            k_dec = k * jnp.exp(g_last - gcol)
            state = st_ref[hd]
            v_new = u_c - _dot(w_c, state)
            o = _dot(qs * eg, state) + _dot(attn, v_new)
            st_ref[hd] = state * jnp.exp(g_last) + _dot_tn(k_dec, v_new)
            o = o * lax.rsqrt(jnp.mean(o * o, axis=-1, keepdims=True) + NORM_EPS) * nw
            o_ref[0, rs, hs] = o * _silu(z_s[rs, hs])
        return carry

    lax.fori_loop(0, nc, chunk_body, 0)


def _seg_sum(x, seg):
    outs = []
    for b in range(x.shape[1] // LANES):
        outs.append(_dot_x2r(x[:, b * LANES:(b + 1) * LANES], seg))
    return jnp.concatenate(outs, axis=1)


def _rwkv_kernel(nc, pad_rows, h_ref, npre_ref, w_ref, mu_ref, w2_ref, a2_ref, vp_ref, o_ref,
                 pbuf, r_s, k_s, v_s, a_s, b_s, lw_s, g_s, y_s, st_ref):
    rows = nc * CHUNK
    t_idx = pl.program_id(1)

    @pl.when(t_idx == 0)
    def _():
        pbuf[0:CARRY, :] = jnp.zeros((CARRY, WB), F32)
        st_ref[...] = jnp.zeros(st_ref.shape, F32)

    hn, _ = _normed_input(h_ref, npre_ref, t_idx, rows, pad_rows)
    pbuf[CARRY:CARRY + rows, :] = jnp.dot(hn, w_ref[...], preferred_element_type=F32)

    def mixed(lo, width):
        cur = pbuf[CARRY:CARRY + rows, lo:lo + width]
        prev = pbuf[CARRY - 1:CARRY - 1 + rows, lo:lo + width]
        return cur + (prev - cur) * mu_ref[:, lo:lo + width]

    w0 = vp_ref[0:1, :]
    a0 = vp_ref[1:2, :]
    k_k = vp_ref[2:3, :]
    k_a = vp_ref[3:4, :]
    r_k = vp_ref[4:5, :]
    ln_w = vp_ref[5:6, :]
    ln_b = vp_ref[6:7, :]

    hr = _iota2((LANES, LANES), 0) // DH_B
    hc = _iota2((LANES, LANES), 1) // DH_B
    seg = jnp.where(hr == hc, 1.0, 0.0).astype(BF16)
    bd = hr == hc

    r = mixed(0, D_B)
    k = mixed(D_B, D_B)
    v = mixed(2 * D_B, D_B)
    gate = mixed(3 * D_B, D_B)
    w_lo = mixed(4 * D_B, LANES)
    a_lo = mixed(4 * D_B + LANES, LANES)
    pbuf[0:CARRY, :] = pbuf[rows:rows + CARRY, :]

    log_w = -jnp.exp(-_softplus(-(w0 + _dot(jnp.tanh(w_lo), w2_ref[...]))) - 0.5)
    a_lr = jax.nn.sigmoid(a0 + _dot(a_lo, a2_ref[...]))
    kk = k * k_k
    kk = kk * lax.rsqrt(_seg_sum(kk * kk, seg) + 1e-6)
    k2 = k * (1.0 + (a_lr - 1.0) * k_a)
    r_s[...] = r
    k_s[...] = k2
    v_s[...] = v
    a_s[...] = -kk
    b_s[...] = kk * a_lr
    lw_s[...] = log_w
    g_s[...] = _silu(gate)

    incl, strict, tri, eye = _tri_masks()
    lane = _iota2((CHUNK, LANES), 1)
    half = (lane < DH_B, lane >= DH_B)

    def chunk_body(c, carry):
        r0 = pl.multiple_of(c * CHUNK, CHUNK)
        rs = pl.ds(r0, CHUNK)
        lw = lw_s[rs, :]
        cl = _dot_lx3(tri, lw)
        cl_last = cl[CHUNK - 1:CHUNK, :]
        e_neg = jnp.exp(-cl)
        e_end = jnp.exp(cl_last - cl)
        a_t = a_s[rs, :] * jnp.exp(cl - lw)
        r_t = r_s[rs, :] * jnp.exp(cl)
        b_c = b_s[rs, :]
        k_c = k_s[rs, :]
        v_c = v_s[rs, :]
        b_t = b_c * e_neg
        k_t = k_c * e_neg
        b_e = b_c * e_end
        k_e = k_c * e_end
        gam = jnp.exp(cl_last)
        for pr in range(H_B // 2):
            ls = slice(pr * LANES, (pr + 1) * LANES)
            state = st_ref[pr]
            vp = v_c[:, ls]
            wm = None
            u0 = None
            rbs, rks = [], []
            for hh in range(2):
                am = jnp.where(half[hh], a_t[:, ls], 0.0)
                rm = jnp.where(half[hh], r_t[:, ls], 0.0)
                ab = jnp.where(strict, _dot_nt(am, b_t[:, ls]), 0.0)
                ak = jnp.where(strict, _dot_nt(am, k_t[:, ls]), 0.0)
                rbs.append(jnp.where(incl, _dot_nt(rm, b_t[:, ls]), 0.0))
                rks.append(jnp.where(incl, _dot_nt(rm, k_t[:, ls]), 0.0))
                t_inv = _neumann_inverse(ab, eye)
                wm_h = _dot3(t_inv, am)
                u0_h = _dot3(t_inv, _dot(ak, vp))
                wm = wm_h if wm is None else wm + wm_h
                u0 = u0_h if u0 is None else jnp.where(half[0], u0, u0_h)
            u = _dot_nt(wm, state) + u0
            y_in = jnp.where(half[0], _dot(rbs[0], u) + _dot(rks[0], vp),
                             _dot(rbs[1], u) + _dot(rks[1], vp))
            y_s[rs, ls] = _dot_nt(r_t[:, ls], state) + y_in
            new_state = state * gam[:, ls] + _dot_tn(u, b_e[:, ls]) + _dot_tn(vp, k_e[:, ls])
            st_ref[pr] = jnp.where(bd, new_state, 0.0)
        return carry

    lax.fori_loop(0, nc, chunk_body, 0)

    y = y_s[...]
    r = r_s[...]
    k2 = k_s[...]
    v = v_s[...]
    mean = _seg_sum(y, seg) * (1.0 / DH_B)
    yc = y - mean
    var = _seg_sum(yc * yc, seg) * (1.0 / DH_B)
    yn = yc * lax.rsqrt(var + GN_EPS_B) * ln_w + ln_b
    bonus = _seg_sum(r * k2 * r_k, seg) * v
    o_ref[0] = (yn + bonus) * g_s[...]


def _mamba_kernel(nc, pad_rows, h_ref, npre_ref, w_ref, cw_ref, cb_ref, hp_ref, vp_ref, o_ref,
                  ubuf, z_s, x_s, bm_s, cm_s, dt_s, a_s, st_ref):
    rows = nc * CHUNK
    t_idx = pl.program_id(1)

    @pl.when(t_idx == 0)
    def _():
        ubuf[0:CARRY, :] = jnp.zeros((CARRY, WC_CONV), F32)
        st_ref[...] = jnp.zeros(st_ref.shape, F32)

    hn, valid = _normed_input(h_ref, npre_ref, t_idx, rows, pad_rows)
    z_s[...] = jnp.dot(hn, w_ref[:, 0:D_CP], preferred_element_type=F32)
    ubuf[CARRY:CARRY + rows, :] = jnp.dot(hn, w_ref[:, D_CP:D_CP + WC_CONV],
                                          preferred_element_type=F32)
    dtp = jnp.dot(hn, w_ref[:, D_CP + WC_CONV:WC], preferred_element_type=F32)

    dt = jnp.where(valid, _softplus(dtp + hp_ref[0:1, :]), 0.0)
    dt_s[...] = dt
    a_s[...] = -jnp.exp(hp_ref[1:2, :]) * dt

    for blk in range(WC_CONV // LANES):
        lo = blk * LANES
        y = _silu(_causal_conv_block(ubuf, cw_ref, lo, rows) + cb_ref[:, lo:lo + LANES])
        if lo < D_CP:
            x_s[:, lo:lo + LANES] = y
        elif lo < D_CP + G_C * N_C:
            bm_s[:, lo - D_CP:lo - D_CP + LANES] = jnp.where(valid, y, 0.0)
        else:
            o2 = lo - D_CP - G_C * N_C
            cm_s[:, o2:o2 + LANES] = jnp.where(valid, y, 0.0)
    ubuf[0:CARRY, :] = ubuf[rows:rows + CARRY, :]

    incl, _, tri, _ = _tri_masks()
    er = _iota2((LANES, D_CP), 0)
    ec = _iota2((LANES, D_CP), 1)
    owner = (ec // GW) * HPG + (ec % GW) // DH_C
    expand = jnp.where((er == owner) & ((ec % GW) < HPG * DH_C), 1.0, 0.0).astype(BF16)
    glane = _iota2((CHUNK, GW), 1)
    prow = _iota2((GW, N_C), 0)
    d_skip = vp_ref[0:1, :]
    nw = vp_ref[1:2, :]

    def chunk_body(c, carry):
        r0 = pl.multiple_of(c * CHUNK, CHUNK)
        rs = pl.ds(r0, CHUNK)
        acs = _dot_lx3(tri, a_s[rs, :])
        acst = acs.T
        acs_x = _dot_x3r(acs, expand)
        dt_x = _dot_x2r(dt_s[rs, :], expand)
        e_in = jnp.exp(acs_x)
        e_out = jnp.exp(acs_x[CHUNK - 1:CHUNK, :] - acs_x)
        xs = x_s[rs, :]
        xdt = xs * dt_x
        for gi in range(G_C):
            gs = slice(gi * GW, (gi + 1) * GW)
            ns = slice(gi * N_C, (gi + 1) * N_C)
            cg = cm_s[rs, ns]
            bg = bm_s[rs, ns]
            cb = _dot_nt(cg, bg)
            xg = xdt[:, gs]
            y = None
            scale_rows = jnp.zeros((GW, N_C), F32)
            for hh in range(HPG):
                hd = gi * HPG + hh
                col = acs[:, hd:hd + 1]
                row = acst[hd:hd + 1, :]
                l_mat = jnp.where(incl, jnp.exp(jnp.where(incl, col - row, 0.0)), 0.0)
                in_head = (glane >= hh * DH_C) & (glane < (hh + 1) * DH_C)
                term = _dot(cb * l_mat, jnp.where(in_head, xg, 0.0))
                y = term if y is None else y + term
                a_last = jnp.exp(acs[CHUNK - 1:CHUNK, hd:hd + 1])
                scale_rows = jnp.where((prow >= hh * DH_C) & (prow < (hh + 1) * DH_C),
                                       a_last, scale_rows)
            state = st_ref[gi]
            y = y + _dot_nt(cg, state) * e_in[:, gs]
            st_ref[gi] = state * scale_rows + _dot_tn(xg * e_out[:, gs], bg)
            y = y + d_skip[:, gs] * xs[:, gs]
            y = y * _silu(z_s[rs, gs])
            ms = jnp.sum(y * y, axis=-1, keepdims=True) * (1.0 / (HPG * DH_C))
            o_ref[0, rs, gs] = y * lax.rsqrt(ms + NORM_EPS) * nw[:, gs]
        return carry

    lax.fori_loop(0, nc, chunk_body, 0)


def _out_kernel(h_ref, oa_ref, ob_ref, oc_ref, wa_ref, wb_ref, wc_ref, npost_ref, o_ref):
    out = (_dot(oa_ref[0], wa_ref[...]) + _dot(ob_ref[0], wb_ref[...])
           + _dot(oc_ref[0], wc_ref[...]))
    ms = jnp.mean(out * out, axis=-1, keepdims=True)
    o_ref[0] = h_ref[0] + out * lax.rsqrt(ms + NORM_EPS) * npost_ref[...]


def _pad_lanes(x, width=LANES):
    return jnp.pad(x, [(0, 0)] * (x.ndim - 1) + [(0, width - x.shape[-1])])


def _group_pad(x):
    lead = x.shape[:-1]
    xg = x.reshape(lead + (G_C, HPG * DH_C))
    xg = jnp.pad(xg, [(0, 0)] * len(lead) + [(0, 0), (0, GW - HPG * DH_C)])
    return xg.reshape(lead + (D_CP,))


def _tiling(seq):
    n_chunks = -(-(N_META + seq) // CHUNK)
    nc = MAX_TILE_CHUNKS
    while (n_chunks + (-n_chunks) % nc) - n_chunks > 1 and nc > 1:
        nc -= 1
    n_chunks += (-n_chunks) % nc
    lp = n_chunks * CHUNK
    return nc, lp - N_META - seq, lp


def _const_spec(shape):
    return pl.BlockSpec(shape, lambda b, t: (0,) * len(shape))


def _mixer_call(body, nc, bsz, lp, out_w, consts, scratch, name, h):
    rows = nc * CHUNK
    tile = pl.BlockSpec((1, rows, D_MODEL), lambda b, t: (b, t, 0))
    return pl.pallas_call(
        body,
        out_shape=jax.ShapeDtypeStruct((bsz, lp, out_w), F32),
        grid=(bsz, lp // rows),
        in_specs=[tile] + [_const_spec(c.shape) for c in consts],
        out_specs=pl.BlockSpec((1, rows, out_w), lambda b, t: (b, t, 0)),
        scratch_shapes=scratch,
        compiler_params=pltpu.CompilerParams(
            dimension_semantics=("arbitrary", "arbitrary"), vmem_limit_bytes=VMEM_LIMIT),
        name=name,
    )(h, *consts)


def kernel(x, meta_tokens, norm_pre, norm_post, w_in, w_out, dn_conv, dn_A_log, dn_dt_bias, dn_norm, rw_mu, rw_w0, rw_w2, rw_a0, rw_a2, rw_k_k, rw_k_a, rw_r_k, rw_ln_w, rw_ln_b, mb_conv, mb_conv_b, mb_dt_bias, mb_A_log, mb_D, mb_norm):
    bsz, seq, _ = x.shape
    depth = w_in.shape[0]
    nc, pad_rows, lp = _tiling(seq)
    rows = nc * CHUNK
    meta = jnp.broadcast_to(meta_tokens.astype(x.dtype)[None], (bsz, N_META, D_MODEL))
    h = jnp.concatenate([jnp.zeros((bsz, pad_rows, D_MODEL), x.dtype), meta, x], axis=1)

    vm = lambda r, c: pltpu.VMEM((r, c), F32)
    for l in range(depth):
        npre = norm_pre[l][None, :]
        wl = w_in[l]
        w_a = jnp.concatenate([wl[:, :4 * D_A], _pad_lanes(wl[:, 4 * D_A:4 * D_A + H_A]),
                               _pad_lanes(wl[:, 4 * D_A + H_A:P_A])], axis=1).astype(BF16)
        hp_a = jnp.stack([_pad_lanes(dn_A_log[l]), _pad_lanes(dn_dt_bias[l])])
        hp_a = jnp.pad(hp_a, ((0, 6), (0, 0)))
        o_a = _mixer_call(
            functools.partial(_deltanet_kernel, nc, pad_rows), nc, bsz, lp, D_A,
            [npre, w_a, dn_conv[l], hp_a, dn_norm[l][None, :]],
            [vm(rows + CARRY, 3 * D_A), vm(rows, D_A), vm(rows, D_A), vm(rows, D_A),
             vm(rows, D_A), vm(rows, 2 * LANES), pltpu.VMEM((H_A, DH_A, DH_A), F32)],
            "deltanet", h)
        wb_l = wl[:, P_A:P_A + P_B]
        w_b = jnp.concatenate([wb_l[:, :4 * D_B], _pad_lanes(wb_l[:, 4 * D_B:4 * D_B + R_W]),
                               _pad_lanes(wb_l[:, 4 * D_B + R_W:])], axis=1).astype(BF16)
        mu = rw_mu[l]
        mu_b = jnp.concatenate([mu[:4 * D_B], _pad_lanes(mu[4 * D_B:4 * D_B + R_W]),
                                _pad_lanes(mu[4 * D_B + R_W:])])[None, :]
        w2p = jnp.pad(rw_w2[l], ((0, LANES - R_W), (0, 0))).astype(BF16)
        a2p = jnp.pad(rw_a2[l], ((0, LANES - R_A), (0, 0))).astype(BF16)
        vp_b = jnp.stack([rw_w0[l], rw_a0[l], rw_k_k[l], rw_k_a[l], rw_r_k[l].reshape(D_B),
                          rw_ln_w[l], rw_ln_b[l], jnp.zeros((D_B,), F32)])
        o_b = _mixer_call(
            functools.partial(_rwkv_kernel, nc, pad_rows), nc, bsz, lp, D_B,
            [npre, w_b, mu_b, w2p, a2p, vp_b],
            [vm(rows + CARRY, WB)] + [vm(rows, D_B)] * 8
            + [pltpu.VMEM((H_B // 2, LANES, LANES), F32)],
            "rwkv7", h)
        wc_l = wl[:, P_A + P_B:]
        w_c = jnp.concatenate([
            _group_pad(wc_l[:, :D_C]), _group_pad(wc_l[:, D_C:2 * D_C]),
            wc_l[:, 2 * D_C:2 * D_C + 2 * G_C * N_C], _pad_lanes(wc_l[:, 2 * D_C + 2 * G_C * N_C:]),
        ], axis=1).astype(BF16)
        cw_c = jnp.concatenate([_group_pad(mb_conv[l][:, :D_C]), mb_conv[l][:, D_C:]], axis=1)
        cb_c = jnp.concatenate([_group_pad(mb_conv_b[l][:D_C]), mb_conv_b[l][D_C:]])[None, :]
        hp_c = jnp.pad(jnp.stack([_pad_lanes(mb_dt_bias[l]), _pad_lanes(mb_A_log[l])]),
                       ((0, 6), (0, 0)))
        vp_c = jnp.pad(jnp.stack([_group_pad(jnp.repeat(mb_D[l], DH_C)), _group_pad(mb_norm[l])]),
                       ((0, 6), (0, 0)))
        o_c = _mixer_call(
            functools.partial(_mamba_kernel, nc, pad_rows), nc, bsz, lp, D_CP,
            [npre, w_c, cw_c, cb_c, hp_c, vp_c],
            [vm(rows + CARRY, WC_CONV), vm(rows, D_CP), vm(rows, D_CP), vm(rows, G_C * N_C),
             vm(rows, G_C * N_C), vm(rows, LANES), vm(rows, LANES),
             pltpu.VMEM((G_C, GW, N_C), F32)],
            "mamba2", h)
        wo = w_out[l]
        wo_a = wo[:D_A].astype(BF16)
        wo_b = wo[D_A:D_A + D_B].astype(BF16)
        wo_c = jnp.pad(wo[D_A + D_B:].reshape(G_C, HPG * DH_C, D_MODEL),
                       ((0, 0), (0, GW - HPG * DH_C), (0, 0))).reshape(D_CP, D_MODEL).astype(BF16)
        row_spec = lambda w: pl.BlockSpec((1, rows, w), lambda b, t: (b, t, 0))
        h = pl.pallas_call(
            _out_kernel,
            out_shape=jax.ShapeDtypeStruct((bsz, lp, D_MODEL), F32),
            grid=(bsz, lp // rows),
            in_specs=[row_spec(D_MODEL), row_spec(D_A), row_spec(D_B), row_spec(D_CP),
                      _const_spec(wo_a.shape), _const_spec(wo_b.shape), _const_spec(wo_c.shape),
                      _const_spec((1, D_MODEL))],
            out_specs=row_spec(D_MODEL),
            compiler_params=pltpu.CompilerParams(
                dimension_semantics=("arbitrary", "arbitrary"), vmem_limit_bytes=VMEM_LIMIT),
            name="out_proj",
        )(h, o_a, o_b, o_c, wo_a, wo_b, wo_c, norm_post[l][None, :])
    return h[:, pad_rows + N_META:]
```

```python
import functools

import jax
import jax.numpy as jnp
from jax import lax
from jax.experimental import pallas as pl
from jax.experimental.pallas import tpu as pltpu

F32 = jnp.float32
BF16 = jnp.bfloat16

D_MODEL = 1024
CHUNK = 64
SB_CHUNKS = 4
SBR = SB_CHUNKS * CHUNK
N_META = 16
CONV_K = 4
NORM_EPS = 1e-6
LANES = 128
CARRY = 8

H_A, DH_A = 6, 128
D_A = H_A * DH_A
H_B, DH_B = 8, 64
D_B = H_B * DH_B
R_W = R_A = 64
GN_EPS_B = 64e-5
H_C, DH_C = 12, 64
D_C = H_C * DH_C
G_C, N_C = 4, 128
HPG = H_C // G_C
GW = 256
D_CP = G_C * GW
P_A = 4 * D_A + 2 * H_A
P_B = 4 * D_B + R_W + R_A
P_C = 2 * D_C + 2 * G_C * N_C + H_C

WA = 4 * D_A + 2 * LANES
WB = 4 * D_B + 2 * LANES
WC_CONV = D_CP + 2 * G_C * N_C
WC = D_CP + WC_CONV + LANES

VMEM_LIMIT = 56 * 1024 * 1024
MAX_TILE_SB = 3


def _dot(a, b):
    return jnp.dot(a.astype(BF16), b.astype(BF16), preferred_element_type=F32)


def _dot_nt(a, b):
    return lax.dot_general(a.astype(BF16), b.astype(BF16), (((1,), (1,)), ((), ())),
                           preferred_element_type=F32)


def _dot_tn(a, b):
    return lax.dot_general(a.astype(BF16), b.astype(BF16), (((0,), (0,)), ((), ())),
                           preferred_element_type=F32)


def _split2(x):
    hi = x.astype(BF16)
    lo = (x - hi.astype(F32)).astype(BF16)
    return hi, lo


def _split3(x):
    hi = x.astype(BF16)
    r = x - hi.astype(F32)
    mid = r.astype(BF16)
    lo = (r - mid.astype(F32)).astype(BF16)
    return hi, mid, lo


def _dot_lx3(l_bf16, x):
    hi, mid, lo = _split3(x)
    d = lambda p: jnp.dot(l_bf16, p, preferred_element_type=F32)
    return d(hi) + d(mid) + d(lo)


def _dot_x2r(x, r_bf16):
    hi, lo = _split2(x)
    d = lambda p: jnp.dot(p, r_bf16, preferred_element_type=F32)
    return d(hi) + d(lo)


def _dot_x3r(x, r_bf16):
    hi, mid, lo = _split3(x)
    d = lambda p: jnp.dot(p, r_bf16, preferred_element_type=F32)
    return d(hi) + d(mid) + d(lo)


def _neumann_inverse(x, eye):
    p = eye + x
    xk = x
    for _ in range(5):
        xk = _dot(xk, xk)
        p = p + _dot(xk, p)
    return p


def _silu(x):
    return x * jax.nn.sigmoid(x)


def _softplus(x):
    return jnp.maximum(x, 0.0) + jnp.log1p(jnp.exp(-jnp.abs(x)))


def _iota2(shape, dim):
    return lax.broadcasted_iota(jnp.int32, shape, dim)


def _tri_masks(n):
    r = _iota2((n, n), 0)
    c = _iota2((n, n), 1)
    same = (r // CHUNK) == (c // CHUNK)
    incl = same & (r >= c)
    strict = same & (r > c)
    tri = jnp.where(incl, 1.0, 0.0).astype(BF16)
    blk = jnp.where(same, 1.0, 0.0).astype(BF16)
    eye = jnp.where(r == c, 1.0, 0.0).astype(F32)
    return incl, strict, tri, blk, eye


def _normed_input(h_ref, npre_ref, t_idx, rows, pad_rows):
    h = h_ref[0]
    ms = jnp.mean(h * h, axis=-1, keepdims=True)
    hn = h * lax.rsqrt(ms + NORM_EPS) * npre_ref[...]
    gid = t_idx * rows + _iota2((rows, 1), 0)
    valid = gid >= pad_rows
    hn = jnp.where(valid, hn, 0.0)
    return hn.astype(BF16), valid


def _causal_conv_block(ubuf, cw_ref, lo, rows):
    acc = None
    for j in range(CONV_K):
        start = CARRY - (CONV_K - 1) + j
        term = ubuf[start:start + rows, lo:lo + LANES] * cw_ref[j:j + 1, lo:lo + LANES]
        acc = term if acc is None else acc + term
    return acc


def _deltanet_kernel(nsb, pad_rows, h_ref, npre_ref, w_ref, cw_ref, hp_ref, nw_ref, o_ref,
                     ubuf, q_s, k_s, v_s, z_s, gb_s, gl_s, u_s, w_s, st_ref):
    rows = nsb * SBR
    t_idx = pl.program_id(1)

    @pl.when(t_idx == 0)
    def _():
        ubuf[0:CARRY, :] = jnp.zeros((CARRY, 3 * D_A), F32)
        st_ref[...] = jnp.zeros(st_ref.shape, F32)

    hn, valid = _normed_input(h_ref, npre_ref, t_idx, rows, pad_rows)
    ubuf[CARRY:CARRY + rows, :] = jnp.dot(hn, w_ref[:, 0:3 * D_A], preferred_element_type=F32)
    z_s[...] = jnp.dot(hn, w_ref[:, 3 * D_A:4 * D_A], preferred_element_type=F32)
    ab = jnp.dot(hn, w_ref[:, 4 * D_A:WA], preferred_element_type=F32)

    a_log = hp_ref[0:1, :]
    dt_bias = hp_ref[1:2, :]
    g = -jnp.exp(a_log) * _softplus(ab[:, 0:LANES] + dt_bias)
    beta = jax.nn.sigmoid(ab[:, LANES:2 * LANES])
    gb_s[:, 0:LANES] = jnp.where(valid, g, 0.0)
    gb_s[:, LANES:2 * LANES] = jnp.where(valid, beta, 0.0)

    for blk_i in range(3 * H_A):
        lo = blk_i * LANES
        y = _silu(_causal_conv_block(ubuf, cw_ref, lo, rows))
        if blk_i < 2 * H_A:
            y = y * lax.rsqrt(jnp.sum(y * y, axis=-1, keepdims=True) + 1e-6)
        dst = (q_s, k_s, v_s)[blk_i // H_A]
        hl = (blk_i % H_A) * LANES
        dst[:, hl:hl + LANES] = y
    ubuf[0:CARRY, :] = ubuf[rows:rows + CARRY, :]

    incl, strict, tri, blk, eye = _tri_masks(SBR)
    scale = DH_A ** -0.5

    def sb_body(sb, carry):
        rs = pl.ds(pl.multiple_of(sb * SBR, SBR), SBR)
        g_sb = gb_s[rs, 0:LANES]
        gc = _dot_lx3(tri, g_sb)
        gl = _dot_lx3(blk, g_sb)
        gl_s[rs, :] = gl
        gct = gc.T
        beta_sb = gb_s[rs, LANES:2 * LANES]
        for hd in range(H_A):
            hs = slice(hd * DH_A, (hd + 1) * DH_A)
            q = q_s[rs, hs]
            k = k_s[rs, hs]
            v = v_s[rs, hs]
            gcol = gc[:, hd:hd + 1]
            grow = gct[hd:hd + 1, :]
            bcol = beta_sb[:, hd:hd + 1]
            dmask = jnp.where(incl, jnp.exp(jnp.where(incl, gcol - grow, 0.0)), 0.0)
            kb = k * bcol
            a_mat = jnp.where(strict, _dot_nt(kb, k) * dmask, 0.0)
            t_inv = _neumann_inverse(-a_mat, eye)
            eg = jnp.exp(gcol)
            sol = _dot(t_inv, jnp.concatenate([v * bcol, kb * eg], axis=1))
            qs = q * scale
            attn = _dot_nt(qs, k) * dmask
            asol = _dot(attn, sol)
            u_s[rs, hs] = sol[:, 0:DH_A]
            w_s[rs, hs] = sol[:, DH_A:2 * DH_A]
            v_s[rs, hs] = asol[:, 0:DH_A]
            q_s[rs, hs] = qs * eg - asol[:, DH_A:2 * DH_A]
            k_s[rs, hs] = k * jnp.exp(gl[:, hd:hd + 1] - gcol)
        return carry

    lax.fori_loop(0, nsb, sb_body, 0)

    def chunk_body(c, carry):
        r0 = pl.multiple_of(c * CHUNK, CHUNK)
        rs = pl.ds(r0, CHUNK)
        for hd in range(H_A):
            hs = slice(hd * DH_A, (hd + 1) * DH_A)
            state = st_ref[hd]
            res = _dot(jnp.concatenate([w_s[rs, hs], q_s[rs, hs]], axis=0), state)
            v_new = u_s[rs, hs] - res[0:CHUNK]
            v_s[rs, hs] = v_s[rs, hs] + res[CHUNK:2 * CHUNK]
            g_tot = jnp.exp(gl_s[pl.ds(r0, 1), hd:hd + 1])
            st_ref[hd] = state * g_tot + _dot_tn(k_s[rs, hs], v_new)
        return carry

    lax.fori_loop(0, nsb * SB_CHUNKS, chunk_body, 0)

    nw = nw_ref[...]
    for hd in range(H_A):
        hs = slice(hd * DH_A, (hd + 1) * DH_A)
        o = v_s[:, hs]
        o = o * lax.rsqrt(jnp.mean(o * o, axis=-1, keepdims=True) + NORM_EPS) * nw
        o_ref[0, :, hs] = o * _silu(z_s[:, hs])


def _seg_sum(x, seg):
    outs = []
    for b in range(x.shape[1] // LANES):
        outs.append(_dot_x2r(x[:, b * LANES:(b + 1) * LANES], seg))
    return jnp.concatenate(outs, axis=1)


def _rwkv_kernel(nsb, pad_rows, h_ref, npre_ref, w_ref, mu_ref, w2_ref, a2_ref, vp_ref, o_ref,
                 pbuf, r_s, k_s, v_s, a_s, b_s, lw_s, g_s, y_s, u0_s, be_s, ke_s, st_ref):
    rows = nsb * SBR
    t_idx = pl.program_id(1)

    @pl.when(t_idx == 0)
    def _():
        pbuf[0:CARRY, :] = jnp.zeros((CARRY, WB), F32)
        st_ref[...] = jnp.zeros(st_ref.shape, F32)

    hn, _ = _normed_input(h_ref, npre_ref, t_idx, rows, pad_rows)
    pbuf[CARRY:CARRY + rows, :] = jnp.dot(hn, w_ref[...], preferred_element_type=F32)

    def mixed(lo, width):
        cur = pbuf[CARRY:CARRY + rows, lo:lo + width]
        prev = pbuf[CARRY - 1:CARRY - 1 + rows, lo:lo + width]
        return cur + (prev - cur) * mu_ref[:, lo:lo + width]

    w0 = vp_ref[0:1, :]
    a0 = vp_ref[1:2, :]
    k_k = vp_ref[2:3, :]
    k_a = vp_ref[3:4, :]
    r_k = vp_ref[4:5, :]
    ln_w = vp_ref[5:6, :]
    ln_b = vp_ref[6:7, :]

    hr = _iota2((LANES, LANES), 0) // DH_B
    hc = _iota2((LANES, LANES), 1) // DH_B
    bd = hr == hc
    seg = jnp.where(bd, 1.0, 0.0).astype(BF16)

    r = mixed(0, D_B)
    k = mixed(D_B, D_B)
    v = mixed(2 * D_B, D_B)
    gate = mixed(3 * D_B, D_B)
    w_lo = mixed(4 * D_B, LANES)
    a_lo = mixed(4 * D_B + LANES, LANES)
    pbuf[0:CARRY, :] = pbuf[rows:rows + CARRY, :]

    log_w = -jnp.exp(-_softplus(-(w0 + _dot(jnp.tanh(w_lo), w2_ref[...]))) - 0.5)
    a_lr = jax.nn.sigmoid(a0 + _dot(a_lo, a2_ref[...]))
    kk = k * k_k
    kk = kk * lax.rsqrt(_seg_sum(kk * kk, seg) + 1e-6)
    k2 = k * (1.0 + (a_lr - 1.0) * k_a)
    r_s[...] = r
    k_s[...] = k2
    v_s[...] = v
    a_s[...] = -kk
    b_s[...] = kk * a_lr
    lw_s[...] = log_w
    g_s[...] = _silu(gate)

    incl, strict, tri, blk, eye = _tri_masks(SBR)
    lane = _iota2((SBR, LANES), 1)
    half = (lane < DH_B, lane >= DH_B)
    lane_c = _iota2((CHUNK, LANES), 1)
    half0_c = lane_c < DH_B

    def sb_body(sb, carry):
        rs = pl.ds(pl.multiple_of(sb * SBR, SBR), SBR)
        lw = lw_s[rs, :]
        cl = _dot_lx3(tri, lw)
        cl_end = _dot_lx3(blk, lw)
        e_neg = jnp.exp(-cl)
        e_end = jnp.exp(cl_end - cl)
        a_t = a_s[rs, :] * jnp.exp(cl - lw)
        r_t = r_s[rs, :] * jnp.exp(cl)
        b_c = b_s[rs, :]
        k_c = k_s[rs, :]
        v_c = v_s[rs, :]
        b_t = b_c * e_neg
        k_t = k_c * e_neg
        be_s[rs, :] = b_c * e_end
        ke_s[rs, :] = k_c * e_end
        lw_s[rs, :] = jnp.exp(cl_end)
        for pr in range(H_B // 2):
            ls = slice(pr * LANES, (pr + 1) * LANES)
            vp = v_c[:, ls]
            bk = jnp.concatenate([b_t[:, ls], k_t[:, ls]], axis=0)
            wm = rw = u0 = yc = None
            for hh in range(2):
                am = jnp.where(half[hh], a_t[:, ls], 0.0)
                rm = jnp.where(half[hh], r_t[:, ls], 0.0)
                quad = _dot_nt(jnp.concatenate([am, rm], axis=0), bk)
                ab = jnp.where(strict, quad[0:SBR, 0:SBR], 0.0)
                ak = jnp.where(strict, quad[0:SBR, SBR:2 * SBR], 0.0)
                rb = jnp.where(incl, quad[SBR:2 * SBR, 0:SBR], 0.0)
                rk = jnp.where(incl, quad[SBR:2 * SBR, SBR:2 * SBR], 0.0)
                t_inv = _neumann_inverse(ab, eye)
                sol = _dot(t_inv, jnp.concatenate([am, _dot(ak, vp)], axis=1))
                rbsol = _dot(rb, sol)
                wm_h = sol[:, 0:LANES]
                u0_h = sol[:, LANES:2 * LANES]
                rw_h = rm + rbsol[:, 0:LANES]
                yc_h = rbsol[:, LANES:2 * LANES] + _dot(rk, vp)
                if hh == 0:
                    wm, rw, u0, yc = wm_h, rw_h, u0_h, yc_h
                else:
                    wm = wm + wm_h
                    rw = rw + rw_h
                    u0 = jnp.where(half[0], u0, u0_h)
                    yc = jnp.where(half[0], yc, yc_h)
            a_s[rs, ls] = wm
            b_s[rs, ls] = rw
            u0_s[rs, ls] = u0
            y_s[rs, ls] = yc
        return carry

    lax.fori_loop(0, nsb, sb_body, 0)

    def chunk_body(c, carry):
        r0 = pl.multiple_of(c * CHUNK, CHUNK)
        rs = pl.ds(r0, CHUNK)
        for pr in range(H_B // 2):
            ls = slice(pr * LANES, (pr + 1) * LANES)
            state = st_ref[pr]
            res = _dot_nt(jnp.concatenate([a_s[rs, ls], b_s[rs, ls]], axis=0), state)
            u = res[0:CHUNK] + u0_s[rs, ls]
            y_s[rs, ls] = y_s[rs, ls] + res[CHUNK:2 * CHUNK]
            upd = _dot_tn(jnp.concatenate([u, v_s[rs, ls]], axis=0),
                          jnp.concatenate([be_s[rs, ls], ke_s[rs, ls]], axis=0))
            gam = lw_s[pl.ds(r0, 1), ls]
            st_ref[pr] = jnp.where(bd, state * gam + upd, 0.0)
        return carry

    lax.fori_loop(0, nsb * SB_CHUNKS, chunk_body, 0)

    y = y_s[...]
    r = r_s[...]
    k2 = k_s[...]
    v = v_s[...]
    mean = _seg_sum(y, seg) * (1.0 / DH_B)
    yc = y - mean
    var = _seg_sum(yc * yc, seg) * (1.0 / DH_B)
    yn = yc * lax.rsqrt(var + GN_EPS_B) * ln_w + ln_b
    bonus = _seg_sum(r * k2 * r_k, seg) * v
    o_ref[0] = (yn + bonus) * g_s[...]
    del half0_c


def _mamba_kernel(nc, pad_rows, h_ref, npre_ref, w_ref, cw_ref, cb_ref, hp_ref, vp_ref, o_ref,
                  ubuf, z_s, x_s, bm_s, cm_s, dt_s, a_s, st_ref):
    rows = nc * CHUNK
    t_idx = pl.program_id(1)

    @pl.when(t_idx == 0)
    def _():
        ubuf[0:CARRY, :] = jnp.zeros((CARRY, WC_CONV), F32)
        st_ref[...] = jnp.zeros(st_ref.shape, F32)

    hn, valid = _normed_input(h_ref, npre_ref, t_idx, rows, pad_rows)
    z_s[...] = jnp.dot(hn, w_ref[:, 0:D_CP], preferred_element_type=F32)
    ubuf[CARRY:CARRY + rows, :] = jnp.dot(hn, w_ref[:, D_CP:D_CP + WC_CONV],
                                          preferred_element_type=F32)
    dtp = jnp.dot(hn, w_ref[:, D_CP + WC_CONV:WC], preferred_element_type=F32)

    dt = jnp.where(valid, _softplus(dtp + hp_ref[0:1, :]), 0.0)
    dt_s[...] = dt
    a_s[...] = -jnp.exp(hp_ref[1:2, :]) * dt

    for blk_i in range(WC_CONV // LANES):
        lo = blk_i * LANES
        y = _silu(_causal_conv_block(ubuf, cw_ref, lo, rows) + cb_ref[:, lo:lo + LANES])
        if lo < D_CP:
            x_s[:, lo:lo + LANES] = y
        elif lo < D_CP + G_C * N_C:
            bm_s[:, lo - D_CP:lo - D_CP + LANES] = jnp.where(valid, y, 0.0)
        else:
            o2 = lo - D_CP - G_C * N_C
            cm_s[:, o2:o2 + LANES] = jnp.where(valid, y, 0.0)
    ubuf[0:CARRY, :] = ubuf[rows:rows + CARRY, :]

    incl, _, tri, _, _ = _tri_masks(CHUNK)
    er = _iota2((LANES, D_CP), 0)
    ec = _iota2((LANES, D_CP), 1)
    owner = (ec // GW) * HPG + (ec % GW) // DH_C
    expand = jnp.where((er == owner) & ((ec % GW) < HPG * DH_C), 1.0, 0.0).astype(BF16)
    glane = _iota2((CHUNK, GW), 1)
    prow = _iota2((GW, N_C), 0)
    d_skip = vp_ref[0:1, :]
    nw = vp_ref[1:2, :]

    def chunk_body(c, carry):
        r0 = pl.multiple_of(c * CHUNK, CHUNK)
        rs = pl.ds(r0, CHUNK)
        acs = _dot_lx3(tri, a_s[rs, :])
        acst = acs.T
        acs_x = _dot_x3r(acs, expand)
        dt_x = _dot_x2r(dt_s[rs, :], expand)
        e_in = jnp.exp(acs_x)
        e_out = jnp.exp(acs_x[CHUNK - 1:CHUNK, :] - acs_x)
        xs = x_s[rs, :]
        xdt = xs * dt_x
        for gi in range(G_C):
            gs = slice(gi * GW, (gi + 1) * GW)
            ns = slice(gi * N_C, (gi + 1) * N_C)
            cg = cm_s[rs, ns]
            bg = bm_s[rs, ns]
            cb = _dot_nt(cg, bg)
            xg = xdt[:, gs]
            y = None
            scale_rows = jnp.zeros((GW, N_C), F32)
            for hh in range(HPG):
                hd = gi * HPG + hh
                col = acs[:, hd:hd + 1]
                row = acst[hd:hd + 1, :]
                l_mat = jnp.where(incl, jnp.exp(jnp.where(incl, col - row, 0.0)), 0.0)
                in_head = (glane >= hh * DH_C) & (glane < (hh + 1) * DH_C)
                term = _dot(cb * l_mat, jnp.where(in_head, xg, 0.0))
                y = term if y is None else y + term
                a_last = jnp.exp(acs[CHUNK - 1:CHUNK, hd:hd + 1])
                scale_rows = jnp.where((prow >= hh * DH_C) & (prow < (hh + 1) * DH_C),
                                       a_last, scale_rows)
            state = st_ref[gi]
            y = y + _dot_nt(cg, state) * e_in[:, gs]
            st_ref[gi] = state * scale_rows + _dot_tn(xg * e_out[:, gs], bg)
            y = y + d_skip[:, gs] * xs[:, gs]
            y = y * _silu(z_s[rs, gs])
            ms = jnp.sum(y * y, axis=-1, keepdims=True) * (1.0 / (HPG * DH_C))
            o_ref[0, rs, gs] = y * lax.rsqrt(ms + NORM_EPS) * nw[:, gs]
        return carry

    lax.fori_loop(0, nc, chunk_body, 0)


def _out_kernel(h_ref, oa_ref, ob_ref, oc_ref, wa_ref, wb_ref, wc_ref, npost_ref, o_ref):
    out = (_dot(oa_ref[0], wa_ref[...]) + _dot(ob_ref[0], wb_ref[...])
           + _dot(oc_ref[0], wc_ref[...]))
    ms = jnp.mean(out * out, axis=-1, keepdims=True)
    o_ref[0] = h_ref[0] + out * lax.rsqrt(ms + NORM_EPS) * npost_ref[...]


def _pad_lanes(x, width=LANES):
    return jnp.pad(x, [(0, 0)] * (x.ndim - 1) + [(0, width - x.shape[-1])])


def _group_pad(x):
    lead = x.shape[:-1]
    xg = x.reshape(lead + (G_C, HPG * DH_C))
    xg = jnp.pad(xg, [(0, 0)] * len(lead) + [(0, 0), (0, GW - HPG * DH_C)])
    return xg.reshape(lead + (D_CP,))


def _tiling(seq):
    n_sb = -(-(N_META + seq) // SBR)
    best = None
    for nsb in range(MAX_TILE_SB, 0, -1):
        total = n_sb + (-n_sb) % nsb
        if best is None or total < best[1]:
            best = (nsb, total)
    nsb, total = best
    lp = total * SBR
    return nsb, lp - N_META - seq, lp


def _const_spec(shape):
    return pl.BlockSpec(shape, lambda b, t: (0,) * len(shape))


def _mixer_call(body, rows, bsz, lp, out_w, consts, scratch, name, h):
    tile = pl.BlockSpec((1, rows, D_MODEL), lambda b, t: (b, t, 0))
    return pl.pallas_call(
        body,
        out_shape=jax.ShapeDtypeStruct((bsz, lp, out_w), F32),
        grid=(bsz, lp // rows),
        in_specs=[tile] + [_const_spec(c.shape) for c in consts],
        out_specs=pl.BlockSpec((1, rows, out_w), lambda b, t: (b, t, 0)),
        scratch_shapes=scratch,
        compiler_params=pltpu.CompilerParams(
            dimension_semantics=("arbitrary", "arbitrary"), vmem_limit_bytes=VMEM_LIMIT),
        name=name,
    )(h, *consts)


def kernel(x, meta_tokens, norm_pre, norm_post, w_in, w_out, dn_conv, dn_A_log, dn_dt_bias, dn_norm, rw_mu, rw_w0, rw_w2, rw_a0, rw_a2, rw_k_k, rw_k_a, rw_r_k, rw_ln_w, rw_ln_b, mb_conv, mb_conv_b, mb_dt_bias, mb_A_log, mb_D, mb_norm):
    bsz, seq, _ = x.shape
    depth = w_in.shape[0]
    nsb, pad_rows, lp = _tiling(seq)
    rows = nsb * SBR
    nc = nsb * SB_CHUNKS
    meta = jnp.broadcast_to(meta_tokens.astype(x.dtype)[None], (bsz, N_META, D_MODEL))
    h = jnp.concatenate([jnp.zeros((bsz, pad_rows, D_MODEL), x.dtype), meta, x], axis=1)

    vm = lambda r, c: pltpu.VMEM((r, c), F32)
    for l in range(depth):
        npre = norm_pre[l][None, :]
        wl = w_in[l]
        w_a = jnp.concatenate([wl[:, :4 * D_A], _pad_lanes(wl[:, 4 * D_A:4 * D_A + H_A]),
                               _pad_lanes(wl[:, 4 * D_A + H_A:P_A])], axis=1).astype(BF16)
        hp_a = jnp.stack([_pad_lanes(dn_A_log[l]), _pad_lanes(dn_dt_bias[l])])
        hp_a = jnp.pad(hp_a, ((0, 6), (0, 0)))
        o_a = _mixer_call(
            functools.partial(_deltanet_kernel, nsb, pad_rows), rows, bsz, lp, D_A,
            [npre, w_a, dn_conv[l], hp_a, dn_norm[l][None, :]],
            [vm(rows + CARRY, 3 * D_A), vm(rows, D_A), vm(rows, D_A), vm(rows, D_A),
             vm(rows, D_A), vm(rows, 2 * LANES), vm(rows, LANES), vm(rows, D_A), vm(rows, D_A),
             pltpu.VMEM((H_A, DH_A, DH_A), F32)],
            "deltanet", h)
        wb_l = wl[:, P_A:P_A + P_B]
        w_b = jnp.concatenate([wb_l[:, :4 * D_B], _pad_lanes(wb_l[:, 4 * D_B:4 * D_B + R_W]),
                               _pad_lanes(wb_l[:, 4 * D_B + R_W:])], axis=1).astype(BF16)
        mu = rw_mu[l]
        mu_b = jnp.concatenate([mu[:4 * D_B], _pad_lanes(mu[4 * D_B:4 * D_B + R_W]),
                                _pad_lanes(mu[4 * D_B + R_W:])])[None, :]
        w2p = jnp.pad(rw_w2[l], ((0, LANES - R_W), (0, 0))).astype(BF16)
        a2p = jnp.pad(rw_a2[l], ((0, LANES - R_A), (0, 0))).astype(BF16)
        vp_b = jnp.stack([rw_w0[l], rw_a0[l], rw_k_k[l], rw_k_a[l], rw_r_k[l].reshape(D_B),
                          rw_ln_w[l], rw_ln_b[l], jnp.zeros((D_B,), F32)])
        o_b = _mixer_call(
            functools.partial(_rwkv_kernel, nsb, pad_rows), rows, bsz, lp, D_B,
            [npre, w_b, mu_b, w2p, a2p, vp_b],
            [vm(rows + CARRY, WB)] + [vm(rows, D_B)] * 11
            + [pltpu.VMEM((H_B // 2, LANES, LANES), F32)],
            "rwkv7", h)
        wc_l = wl[:, P_A + P_B:]
        w_c = jnp.concatenate([
            _group_pad(wc_l[:, :D_C]), _group_pad(wc_l[:, D_C:2 * D_C]),
            wc_l[:, 2 * D_C:2 * D_C + 2 * G_C * N_C], _pad_lanes(wc_l[:, 2 * D_C + 2 * G_C * N_C:]),
        ], axis=1).astype(BF16)
        cw_c = jnp.concatenate([_group_pad(mb_conv[l][:, :D_C]), mb_conv[l][:, D_C:]], axis=1)
        cb_c = jnp.concatenate([_group_pad(mb_conv_b[l][:D_C]), mb_conv_b[l][D_C:]])[None, :]
        hp_c = jnp.pad(jnp.stack([_pad_lanes(mb_dt_bias[l]), _pad_lanes(mb_A_log[l])]),
                       ((0, 6), (0, 0)))
        vp_c = jnp.pad(jnp.stack([_group_pad(jnp.repeat(mb_D[l], DH_C)), _group_pad(mb_norm[l])]),
                       ((0, 6), (0, 0)))
        o_c = _mixer_call(
            functools.partial(_mamba_kernel, nc, pad_rows), rows, bsz, lp, D_CP,
            [npre, w_c, cw_c, cb_c, hp_c, vp_c],
            [vm(rows + CARRY, WC_CONV), vm(rows, D_CP), vm(rows, D_CP), vm(rows, G_C * N_C),
             vm(rows, G_C * N_C), vm(rows, LANES), vm(rows, LANES),
             pltpu.VMEM((G_C, GW, N_C), F32)],
            "mamba2", h)
        wo = w_out[l]
        wo_a = wo[:D_A].astype(BF16)
        wo_b = wo[D_A:D_A + D_B].astype(BF16)
        wo_c = jnp.pad(wo[D_A + D_B:].reshape(G_C, HPG * DH_C, D_MODEL),
                       ((0, 0), (0, GW - HPG * DH_C), (0, 0))).reshape(D_CP, D_MODEL).astype(BF16)
        row_spec = lambda w: pl.BlockSpec((1, rows, w), lambda b, t: (b, t, 0))
        h = pl.pallas_call(
            _out_kernel,
            out_shape=jax.ShapeDtypeStruct((bsz, lp, D_MODEL), F32),
            grid=(bsz, lp // rows),
            in_specs=[row_spec(D_MODEL), row_spec(D_A), row_spec(D_B), row_spec(D_CP),
                      _const_spec(wo_a.shape), _const_spec(wo_b.shape), _const_spec(wo_c.shape),
                      _const_spec((1, D_MODEL))],
            out_specs=row_spec(D_MODEL),
            compiler_params=pltpu.CompilerParams(
                dimension_semantics=("arbitrary", "arbitrary"), vmem_limit_bytes=VMEM_LIMIT),
            name="out_proj",
        )(h, o_a, o_b, o_c, wo_a, wo_b, wo_c, norm_post[l][None, :])
    return h[:, pad_rows + N_META:]
```

```python
import functools

import jax
import jax.numpy as jnp
from jax import lax
from jax.experimental import pallas as pl
from jax.experimental.pallas import tpu as pltpu

F32 = jnp.float32
BF16 = jnp.bfloat16

D_MODEL = 1024
CHUNK = 64
SB_CHUNKS = 4
SBR = SB_CHUNKS * CHUNK
N_META = 16
CONV_K = 4
NORM_EPS = 1e-6
LANES = 128
CARRY = 8

H_A, DH_A = 6, 128
D_A = H_A * DH_A
H_B, DH_B = 8, 64
D_B = H_B * DH_B
R_W = R_A = 64
GN_EPS_B = 64e-5
H_C, DH_C = 12, 64
D_C = H_C * DH_C
G_C, N_C = 4, 128
HPG = H_C // G_C
GW = 256
D_CP = G_C * GW
P_A = 4 * D_A + 2 * H_A
P_B = 4 * D_B + R_W + R_A
P_C = 2 * D_C + 2 * G_C * N_C + H_C

WA = 4 * D_A + 2 * LANES
WB = 4 * D_B + 2 * LANES
WC_CONV = D_CP + 2 * G_C * N_C
WC = D_CP + WC_CONV + LANES

VMEM_LIMIT = 56 * 1024 * 1024
MAX_TILE_SB = 3
RWKV_GROUP = 4
UNROLL_SB = False
UNROLL_CHUNKS = True


def _dot(a, b):
    return jnp.dot(a.astype(BF16), b.astype(BF16), preferred_element_type=F32)


def _dot_nt(a, b):
    return lax.dot_general(a.astype(BF16), b.astype(BF16), (((1,), (1,)), ((), ())),
                           preferred_element_type=F32)


def _dot_tn(a, b):
    return lax.dot_general(a.astype(BF16), b.astype(BF16), (((0,), (0,)), ((), ())),
                           preferred_element_type=F32)


def _split2(x):
    hi = x.astype(BF16)
    lo = (x - hi.astype(F32)).astype(BF16)
    return hi, lo


def _split3(x):
    hi = x.astype(BF16)
    r = x - hi.astype(F32)
    mid = r.astype(BF16)
    lo = (r - mid.astype(F32)).astype(BF16)
    return hi, mid, lo


def _dot_lx3(l_bf16, x):
    hi, mid, lo = _split3(x)
    d = lambda p: jnp.dot(l_bf16, p, preferred_element_type=F32)
    return d(hi) + d(mid) + d(lo)


def _dot_x2r(x, r_bf16):
    hi, lo = _split2(x)
    d = lambda p: jnp.dot(p, r_bf16, preferred_element_type=F32)
    return d(hi) + d(lo)


def _dot_x3r(x, r_bf16):
    hi, mid, lo = _split3(x)
    d = lambda p: jnp.dot(p, r_bf16, preferred_element_type=F32)
    return d(hi) + d(mid) + d(lo)


def _to_wide(m, lane_blk):
    s = [m[c * CHUNK:(c + 1) * CHUNK, :] for c in range(SB_CHUNKS)]
    out = s[SB_CHUNKS - 1]
    for c in range(SB_CHUNKS - 2, -1, -1):
        out = jnp.where(lane_blk == c, s[c], out)
    return out


def _to_block_diag(w, lane_blk):
    wb = w.astype(BF16)
    zero = jnp.zeros_like(wb)
    return jnp.concatenate([jnp.where(lane_blk == c, wb, zero) for c in range(SB_CHUNKS)], axis=0)


def _neumann_inverses(xs, eye_w, lane_blk):
    bd = lambda w: _to_block_diag(w, lane_blk)
    mm = lambda a, b: jnp.dot(a.astype(BF16), b, preferred_element_type=F32)
    xw = [_to_wide(x, lane_blk) for x in xs]
    p = [eye_w + w for w in xw]
    xk = [mm(w, bd(w)) for w in xw]
    for _ in range(4):
        both = [mm(a, jnp.concatenate([bd(a), bd(b)], axis=1)) for a, b in zip(xk, p)]
        xk = [t[:, 0:SBR] for t in both]
        p = [b + t[:, SBR:2 * SBR] for b, t in zip(p, both)]
    p = [b + mm(a, bd(b)) for a, b in zip(xk, p)]
    return [bd(b) for b in p]


def _rows(i, n):
    if isinstance(i, int):
        return slice(i * n, (i + 1) * n)
    return pl.ds(pl.multiple_of(i * n, n), n)


def _row(i, n):
    if isinstance(i, int):
        return slice(i * n, i * n + 1)
    return pl.ds(pl.multiple_of(i * n, n), 1)


def _repeat(n, body, unroll):
    if unroll:
        for i in range(n):
            body(i, 0)
    else:
        lax.fori_loop(0, n, body, 0)


def _silu(x):
    return x * jax.nn.sigmoid(x)


def _softplus(x):
    return jnp.maximum(x, 0.0) + jnp.log1p(jnp.exp(-jnp.abs(x)))


def _iota2(shape, dim):
    return lax.broadcasted_iota(jnp.int32, shape, dim)


def _tri_masks(n):
    r = _iota2((n, n), 0)
    c = _iota2((n, n), 1)
    same = (r // CHUNK) == (c // CHUNK)
    incl = same & (r >= c)
    strict = same & (r > c)
    tri = jnp.where(incl, 1.0, 0.0).astype(BF16)
    blk = jnp.where(same, 1.0, 0.0).astype(BF16)
    return incl, strict, tri, blk


def _wide_consts():
    r = _iota2((CHUNK, SBR), 0)
    c = _iota2((CHUNK, SBR), 1)
    return jnp.where(c % CHUNK == r, 1.0, 0.0).astype(F32), c // CHUNK


def _normed_input(h_ref, npre_ref, t_idx, rows, pad_rows):
    h = h_ref[0]
    ms = jnp.mean(h * h, axis=-1, keepdims=True)
    hn = h * lax.rsqrt(ms + NORM_EPS) * npre_ref[...]
    gid = t_idx * rows + _iota2((rows, 1), 0)
    valid = gid >= pad_rows
    hn = jnp.where(valid, hn, 0.0)
    return hn.astype(BF16), valid


def _causal_conv_block(ubuf, cw_ref, lo, rows):
    full = ubuf[0:CARRY + rows, lo:lo + LANES]
    acc = full[CARRY:] * cw_ref[CONV_K - 1:CONV_K, lo:lo + LANES]
    for s in range(1, CONV_K):
        tap = cw_ref[CONV_K - 1 - s:CONV_K - s, lo:lo + LANES]
        acc = acc + pltpu.roll(full, s, axis=0)[CARRY:] * tap
    return acc


def _deltanet_kernel(nsb, pad_rows, h_ref, npre_ref, w_ref, cw_ref, hp_ref, nw_ref, o_ref,
                     ubuf, q_s, k_s, v_s, z_s, gb_s, gl_s, u_s, w_s, st_ref):
    rows = nsb * SBR
    t_idx = pl.program_id(1)

    @pl.when(t_idx == 0)
    def _():
        ubuf[0:CARRY, :] = jnp.zeros((CARRY, 3 * D_A), F32)
        st_ref[...] = jnp.zeros(st_ref.shape, F32)

    hn, valid = _normed_input(h_ref, npre_ref, t_idx, rows, pad_rows)
    ubuf[CARRY:CARRY + rows, :] = jnp.dot(hn, w_ref[:, 0:3 * D_A], preferred_element_type=F32)
    z_s[...] = jnp.dot(hn, w_ref[:, 3 * D_A:4 * D_A], preferred_element_type=F32)
    ab = jnp.dot(hn, w_ref[:, 4 * D_A:WA], preferred_element_type=F32)

    a_log = hp_ref[0:1, :]
    dt_bias = hp_ref[1:2, :]
    g = -jnp.exp(a_log) * _softplus(ab[:, 0:LANES] + dt_bias)
    beta = jax.nn.sigmoid(ab[:, LANES:2 * LANES])
    gb_s[:, 0:LANES] = jnp.where(valid, g, 0.0)
    gb_s[:, LANES:2 * LANES] = jnp.where(valid, beta, 0.0)

    for blk_i in range(3 * H_A):
        lo = blk_i * LANES
        y = _silu(_causal_conv_block(ubuf, cw_ref, lo, rows))
        if blk_i < 2 * H_A:
            y = y * lax.rsqrt(jnp.sum(y * y, axis=-1, keepdims=True) + 1e-6)
        dst = (q_s, k_s, v_s)[blk_i // H_A]
        hl = (blk_i % H_A) * LANES
        dst[:, hl:hl + LANES] = y
    ubuf[0:CARRY, :] = ubuf[rows:rows + CARRY, :]

    incl, strict, tri, blk = _tri_masks(SBR)
    eye_w, lane_blk = _wide_consts()
    scale = DH_A ** -0.5

    def sb_body(sb, carry):
        rs = _rows(sb, SBR)
        g_sb = gb_s[rs, 0:LANES]
        gc = _dot_lx3(tri, g_sb)
        gl = _dot_lx3(blk, g_sb)
        gl_s[rs, :] = gl
        gct = gc.T
        beta_sb = gb_s[rs, LANES:2 * LANES]
        heads = range(H_A)
        hs = [slice(hd * DH_A, (hd + 1) * DH_A) for hd in heads]
        k = [k_s[rs, hs[hd]] for hd in heads]
        gcol = [gc[:, hd:hd + 1] for hd in heads]
        bcol = [beta_sb[:, hd:hd + 1] for hd in heads]
        dmask = [jnp.where(incl, jnp.exp(jnp.where(incl, gcol[hd] - gct[hd:hd + 1, :], 0.0)), 0.0)
                 for hd in heads]
        kb = [k[hd] * bcol[hd] for hd in heads]
        a_neg = [jnp.where(strict, -(_dot_nt(kb[hd], k[hd]) * dmask[hd]), 0.0) for hd in heads]
        t_inv = _neumann_inverses(a_neg, eye_w, lane_blk)
        eg = [jnp.exp(gcol[hd]) for hd in heads]
        sol = [_dot(t_inv[hd], jnp.concatenate([v_s[rs, hs[hd]] * bcol[hd], kb[hd] * eg[hd]], axis=1))
               for hd in heads]
        qs = [q_s[rs, hs[hd]] * scale for hd in heads]
        attn = [_dot_nt(qs[hd], k[hd]) * dmask[hd] for hd in heads]
        asol = [_dot(attn[hd], sol[hd]) for hd in heads]
        for hd in heads:
            u_s[rs, hs[hd]] = sol[hd][:, 0:DH_A]
            w_s[rs, hs[hd]] = sol[hd][:, DH_A:2 * DH_A]
            v_s[rs, hs[hd]] = asol[hd][:, 0:DH_A]
            q_s[rs, hs[hd]] = qs[hd] * eg[hd] - asol[hd][:, DH_A:2 * DH_A]
            k_s[rs, hs[hd]] = k[hd] * jnp.exp(gl[:, hd:hd + 1] - gcol[hd])
        return carry

    _repeat(nsb, sb_body, UNROLL_SB)

    def chunk_body(c, carry):
        rs = _rows(c, CHUNK)
        heads = range(H_A)
        hs = [slice(hd * DH_A, (hd + 1) * DH_A) for hd in heads]
        state = [st_ref[hd] for hd in heads]
        res = [_dot(jnp.concatenate([w_s[rs, hs[hd]], q_s[rs, hs[hd]]], axis=0), state[hd])
               for hd in heads]
        v_new = [u_s[rs, hs[hd]] - res[hd][0:CHUNK] for hd in heads]
        upd = [_dot_tn(k_s[rs, hs[hd]], v_new[hd]) for hd in heads]
        for hd in heads:
            v_s[rs, hs[hd]] = v_s[rs, hs[hd]] + res[hd][CHUNK:2 * CHUNK]
            g_tot = jnp.exp(gl_s[_row(c, CHUNK), hd:hd + 1])
            st_ref[hd] = state[hd] * g_tot + upd[hd]
        return carry

    _repeat(nsb * SB_CHUNKS, chunk_body, UNROLL_CHUNKS)

    nw = nw_ref[...]
    for hd in range(H_A):
        hs = slice(hd * DH_A, (hd + 1) * DH_A)
        o = v_s[:, hs]
        o = o * lax.rsqrt(jnp.mean(o * o, axis=-1, keepdims=True) + NORM_EPS) * nw
        o_ref[0, :, hs] = o * _silu(z_s[:, hs])


def _seg_sum(x, seg):
    outs = []
    for b in range(x.shape[1] // LANES):
        outs.append(_dot_x2r(x[:, b * LANES:(b + 1) * LANES], seg))
    return jnp.concatenate(outs, axis=1)


def _rwkv_kernel(nsb, pad_rows, h_ref, npre_ref, w_ref, mu_ref, w2_ref, a2_ref, vp_ref, o_ref,
                 pbuf, r_s, k_s, v_s, a_s, b_s, lw_s, g_s, y_s, u0_s, be_s, ke_s, st_ref):
    rows = nsb * SBR
    t_idx = pl.program_id(1)

    @pl.when(t_idx == 0)
    def _():
        pbuf[0:CARRY, :] = jnp.zeros((CARRY, WB), F32)
        st_ref[...] = jnp.zeros(st_ref.shape, F32)

    hn, _ = _normed_input(h_ref, npre_ref, t_idx, rows, pad_rows)
    pbuf[CARRY:CARRY + rows, :] = jnp.dot(hn, w_ref[...], preferred_element_type=F32)

    def mixed(lo, width):
        full = pbuf[0:CARRY + rows, lo:lo + width]
        cur = full[CARRY:]
        prev = pltpu.roll(full, 1, axis=0)[CARRY:]
        return cur + (prev - cur) * mu_ref[:, lo:lo + width]

    w0 = vp_ref[0:1, :]
    a0 = vp_ref[1:2, :]
    k_k = vp_ref[2:3, :]
    k_a = vp_ref[3:4, :]
    r_k = vp_ref[4:5, :]
    ln_w = vp_ref[5:6, :]
    ln_b = vp_ref[6:7, :]

    hr = _iota2((LANES, LANES), 0) // DH_B
    hc = _iota2((LANES, LANES), 1) // DH_B
    bd = hr == hc
    seg = jnp.where(bd, 1.0, 0.0).astype(BF16)

    r = mixed(0, D_B)
    k = mixed(D_B, D_B)
    v = mixed(2 * D_B, D_B)
    gate = mixed(3 * D_B, D_B)
    w_lo = mixed(4 * D_B, LANES)
    a_lo = mixed(4 * D_B + LANES, LANES)
    pbuf[0:CARRY, :] = pbuf[rows:rows + CARRY, :]

    log_w = -jnp.exp(-_softplus(-(w0 + _dot(jnp.tanh(w_lo), w2_ref[...]))) - 0.5)
    a_lr = jax.nn.sigmoid(a0 + _dot(a_lo, a2_ref[...]))
    kk = k * k_k
    kk = kk * lax.rsqrt(_seg_sum(kk * kk, seg) + 1e-6)
    k2 = k * (1.0 + (a_lr - 1.0) * k_a)
    r_s[...] = r
    k_s[...] = k2
    v_s[...] = v
    a_s[...] = -kk
    b_s[...] = kk * a_lr
    lw_s[...] = log_w
    g_s[...] = _silu(gate)

    incl, strict, tri, blk = _tri_masks(SBR)
    eye_w, lane_blk = _wide_consts()
    lane = _iota2((SBR, LANES), 1)
    half = (lane < DH_B, lane >= DH_B)
    lane_c = _iota2((CHUNK, LANES), 1)
    half0_c = lane_c < DH_B

    def sb_body(sb, carry):
        rs = _rows(sb, SBR)
        lw = lw_s[rs, :]
        cl = _dot_lx3(tri, lw)
        cl_end = _dot_lx3(blk, lw)
        e_neg = jnp.exp(-cl)
        e_end = jnp.exp(cl_end - cl)
        a_t = a_s[rs, :] * jnp.exp(cl - lw)
        r_t = r_s[rs, :] * jnp.exp(cl)
        b_c = b_s[rs, :]
        k_c = k_s[rs, :]
        v_c = v_s[rs, :]
        b_t = b_c * e_neg
        k_t = k_c * e_neg
        be_s[rs, :] = b_c * e_end
        ke_s[rs, :] = k_c * e_end
        lw_s[rs, :] = jnp.exp(cl_end)
        for first in range(0, H_B, RWKV_GROUP):
            rwkv_heads(rs, range(first, first + RWKV_GROUP), a_t, r_t, b_t, k_t, v_c)
        return carry

    def rwkv_heads(rs, heads, a_t, r_t, b_t, k_t, v_c):
        ls = {hd: slice((hd // 2) * LANES, (hd // 2 + 1) * LANES) for hd in heads}
        am = {hd: jnp.where(half[hd % 2], a_t[:, ls[hd]], 0.0) for hd in heads}
        rm = {hd: jnp.where(half[hd % 2], r_t[:, ls[hd]], 0.0) for hd in heads}
        quad = {hd: _dot_nt(jnp.concatenate([am[hd], rm[hd]], axis=0),
                            jnp.concatenate([b_t[:, ls[hd]], k_t[:, ls[hd]]], axis=0))
                for hd in heads}
        ab = [jnp.where(strict, quad[hd][0:SBR, 0:SBR], 0.0) for hd in heads]
        t_inv = dict(zip(heads, _neumann_inverses(ab, eye_w, lane_blk)))
        akv = {hd: _dot(jnp.where(strict, quad[hd][0:SBR, SBR:2 * SBR], 0.0), v_c[:, ls[hd]])
               for hd in heads}
        sol = {hd: _dot(t_inv[hd], jnp.concatenate([am[hd], akv[hd]], axis=1)) for hd in heads}
        rbsol = {hd: _dot(jnp.where(incl, quad[hd][SBR:2 * SBR, 0:SBR], 0.0), sol[hd])
                 for hd in heads}
        rkv = {hd: _dot(jnp.where(incl, quad[hd][SBR:2 * SBR, SBR:2 * SBR], 0.0), v_c[:, ls[hd]])
               for hd in heads}
        for h0 in list(heads)[0::2]:
            h1 = h0 + 1
            a_s[rs, ls[h0]] = sol[h0][:, 0:LANES] + sol[h1][:, 0:LANES]
            b_s[rs, ls[h0]] = rm[h0] + rbsol[h0][:, 0:LANES] + rm[h1] + rbsol[h1][:, 0:LANES]
            u0_s[rs, ls[h0]] = jnp.where(half[0], sol[h0][:, LANES:2 * LANES],
                                         sol[h1][:, LANES:2 * LANES])
            y_s[rs, ls[h0]] = jnp.where(half[0], rbsol[h0][:, LANES:2 * LANES] + rkv[h0],
                                        rbsol[h1][:, LANES:2 * LANES] + rkv[h1])

    _repeat(nsb, sb_body, UNROLL_SB)

    def chunk_body(c, carry):
        rs = _rows(c, CHUNK)
        pairs = range(H_B // 2)
        ls = [slice(pr * LANES, (pr + 1) * LANES) for pr in pairs]
        state = [st_ref[pr] for pr in pairs]
        res = [_dot_nt(jnp.concatenate([a_s[rs, ls[pr]], b_s[rs, ls[pr]]], axis=0), state[pr])
               for pr in pairs]
        u = [res[pr][0:CHUNK] + u0_s[rs, ls[pr]] for pr in pairs]
        upd = [_dot_tn(jnp.concatenate([u[pr], v_s[rs, ls[pr]]], axis=0),
                       jnp.concatenate([be_s[rs, ls[pr]], ke_s[rs, ls[pr]]], axis=0))
               for pr in pairs]
        for pr in pairs:
            y_s[rs, ls[pr]] = y_s[rs, ls[pr]] + res[pr][CHUNK:2 * CHUNK]
            gam = lw_s[_row(c, CHUNK), ls[pr]]
            st_ref[pr] = jnp.where(bd, state[pr] * gam + upd[pr], 0.0)
        return carry

    _repeat(nsb * SB_CHUNKS, chunk_body, UNROLL_CHUNKS)

    y = y_s[...]
    r = r_s[...]
    k2 = k_s[...]
    v = v_s[...]
    mean = _seg_sum(y, seg) * (1.0 / DH_B)
    yc = y - mean
    var = _seg_sum(yc * yc, seg) * (1.0 / DH_B)
    yn = yc * lax.rsqrt(var + GN_EPS_B) * ln_w + ln_b
    bonus = _seg_sum(r * k2 * r_k, seg) * v
    o_ref[0] = (yn + bonus) * g_s[...]
    del half0_c


def _mamba_kernel(nc, pad_rows, h_ref, npre_ref, w_ref, cw_ref, cb_ref, hp_ref, vp_ref, o_ref,
                  ubuf, z_s, x_s, bm_s, cm_s, dt_s, a_s, e_in_s, st_ref):
    rows = nc * CHUNK
    t_idx = pl.program_id(1)

    @pl.when(t_idx == 0)
    def _():
        ubuf[0:CARRY, :] = jnp.zeros((CARRY, WC_CONV), F32)
        st_ref[...] = jnp.zeros(st_ref.shape, F32)

    hn, valid = _normed_input(h_ref, npre_ref, t_idx, rows, pad_rows)
    z_s[...] = jnp.dot(hn, w_ref[:, 0:D_CP], preferred_element_type=F32)
    ubuf[CARRY:CARRY + rows, :] = jnp.dot(hn, w_ref[:, D_CP:D_CP + WC_CONV],
                                          preferred_element_type=F32)
    dtp = jnp.dot(hn, w_ref[:, D_CP + WC_CONV:WC], preferred_element_type=F32)

    dt = jnp.where(valid, _softplus(dtp + hp_ref[0:1, :]), 0.0)
    dt_s[...] = dt
    a_s[...] = -jnp.exp(hp_ref[1:2, :]) * dt

    for blk_i in range(WC_CONV // LANES):
        lo = blk_i * LANES
        y = _silu(_causal_conv_block(ubuf, cw_ref, lo, rows) + cb_ref[:, lo:lo + LANES])
        if lo < D_CP:
            x_s[:, lo:lo + LANES] = y
        elif lo < D_CP + G_C * N_C:
            bm_s[:, lo - D_CP:lo - D_CP + LANES] = jnp.where(valid, y, 0.0)
        else:
            o2 = lo - D_CP - G_C * N_C
            cm_s[:, o2:o2 + LANES] = jnp.where(valid, y, 0.0)
    ubuf[0:CARRY, :] = ubuf[rows:rows + CARRY, :]

    incl, _, tri, blk = _tri_masks(SBR)
    glane = _iota2((SBR, GW), 1)
    lane = _iota2((SBR, LANES), 1)
    prow = _iota2((GW, N_C), 0)
    d_skip = vp_ref[0:1, :]

    def expand_heads(f):
        outs = []
        for b in range(D_CP // LANES):
            gi, hb = divmod(b, GW // LANES)
            h0 = gi * HPG + 2 * hb
            second = f[:, h0 + 1:h0 + 2] if hb == 0 else 0.0
            outs.append(jnp.where(lane < DH_C, f[:, h0:h0 + 1], second))
        return jnp.concatenate(outs, axis=1)

    def sb_body(sb, carry):
        rs = _rows(sb, SBR)
        a_sb = a_s[rs, :]
        acs = _dot_lx3(tri, a_sb)
        a_end = _dot_lx3(blk, a_sb)
        dt_sb = dt_s[rs, :]
        acst = acs.T
        dtt = dt_sb.T
        e_in_s[rs, :] = expand_heads(jnp.exp(acs))
        a_s[rs, :] = jnp.exp(a_end)
        xs = x_s[rs, :]
        x_s[rs, :] = xs * expand_heads(dt_sb * jnp.exp(a_end - acs))
        groups = range(G_C)
        gs = [slice(gi * GW, (gi + 1) * GW) for gi in groups]
        cb = [_dot_nt(cm_s[rs, gi * N_C:(gi + 1) * N_C], bm_s[rs, gi * N_C:(gi + 1) * N_C])
              for gi in groups]
        m = []
        for hd in range(H_C):
            decay = jnp.exp(jnp.where(incl, acs[:, hd:hd + 1] - acst[hd:hd + 1, :], 0.0))
            m.append(jnp.where(incl, cb[hd // HPG] * decay, 0.0) * dtt[hd:hd + 1, :])
        in_head = [(glane >= hh * DH_C) & (glane < (hh + 1) * DH_C) for hh in range(HPG)]
        terms = [_dot(m[hd], jnp.where(in_head[hd % HPG], xs[:, gs[hd // HPG]], 0.0))
                 for hd in range(H_C)]
        for gi in groups:
            y = d_skip[:, gs[gi]] * xs[:, gs[gi]]
            for hh in range(HPG):
                y = y + terms[gi * HPG + hh]
            o_ref[0, rs, gs[gi]] = y
        return carry

    _repeat(nc // SB_CHUNKS, sb_body, UNROLL_SB)

    def chunk_body(c, carry):
        rs = _rows(c, CHUNK)
        groups = range(G_C)
        gs = [slice(gi * GW, (gi + 1) * GW) for gi in groups]
        ns = [slice(gi * N_C, (gi + 1) * N_C) for gi in groups]
        state = [st_ref[gi] for gi in groups]
        y_off = [_dot_nt(cm_s[rs, ns[gi]], state[gi]) for gi in groups]
        upd = [_dot_tn(x_s[rs, gs[gi]], bm_s[rs, ns[gi]]) for gi in groups]
        for gi in groups:
            o_ref[0, rs, gs[gi]] = o_ref[0, rs, gs[gi]] + y_off[gi] * e_in_s[rs, gs[gi]]
            scale_rows = jnp.zeros((GW, N_C), F32)
            for hh in range(HPG):
                hd = gi * HPG + hh
                scale_rows = jnp.where((prow >= hh * DH_C) & (prow < (hh + 1) * DH_C),
                                       a_s[_row(c, CHUNK), hd:hd + 1], scale_rows)
            st_ref[gi] = state[gi] * scale_rows + upd[gi]
        return carry

    _repeat(nc, chunk_body, UNROLL_CHUNKS)

    nw = vp_ref[1:2, :]
    for gi in range(G_C):
        gs = slice(gi * GW, (gi + 1) * GW)
        y = o_ref[0, :, gs] * _silu(z_s[:, gs])
        ms = jnp.sum(y * y, axis=-1, keepdims=True) * (1.0 / (HPG * DH_C))
        o_ref[0, :, gs] = y * lax.rsqrt(ms + NORM_EPS) * nw[:, gs]


def _out_kernel(h_ref, oa_ref, ob_ref, oc_ref, wa_ref, wb_ref, wc_ref, npost_ref, o_ref):
    out = (_dot(oa_ref[0], wa_ref[...]) + _dot(ob_ref[0], wb_ref[...])
           + _dot(oc_ref[0], wc_ref[...]))
    ms = jnp.mean(out * out, axis=-1, keepdims=True)
    o_ref[0] = h_ref[0] + out * lax.rsqrt(ms + NORM_EPS) * npost_ref[...]


def _pad_lanes(x, width=LANES):
    return jnp.pad(x, [(0, 0)] * (x.ndim - 1) + [(0, width - x.shape[-1])])


def _group_pad(x):
    lead = x.shape[:-1]
    xg = x.reshape(lead + (G_C, HPG * DH_C))
    xg = jnp.pad(xg, [(0, 0)] * len(lead) + [(0, 0), (0, GW - HPG * DH_C)])
    return xg.reshape(lead + (D_CP,))


def _tiling(seq):
    n_sb = -(-(N_META + seq) // SBR)
    best = None
    for nsb in range(MAX_TILE_SB, 0, -1):
        total = n_sb + (-n_sb) % nsb
        if best is None or total < best[1]:
            best = (nsb, total)
    nsb, total = best
    lp = total * SBR
    return nsb, lp - N_META - seq, lp


def _const_spec(shape):
    return pl.BlockSpec(shape, lambda b, t: (0,) * len(shape))


def _mixer_call(body, rows, bsz, lp, out_w, consts, scratch, name, h):
    tile = pl.BlockSpec((1, rows, D_MODEL), lambda b, t: (b, t, 0))
    return pl.pallas_call(
        body,
        out_shape=jax.ShapeDtypeStruct((bsz, lp, out_w), F32),
        grid=(bsz, lp // rows),
        in_specs=[tile] + [_const_spec(c.shape) for c in consts],
        out_specs=pl.BlockSpec((1, rows, out_w), lambda b, t: (b, t, 0)),
        scratch_shapes=scratch,
        compiler_params=pltpu.CompilerParams(
            dimension_semantics=("arbitrary", "arbitrary"), vmem_limit_bytes=VMEM_LIMIT),
        name=name,
    )(h, *consts)


def kernel(x, meta_tokens, norm_pre, norm_post, w_in, w_out, dn_conv, dn_A_log, dn_dt_bias, dn_norm, rw_mu, rw_w0, rw_w2, rw_a0, rw_a2, rw_k_k, rw_k_a, rw_r_k, rw_ln_w, rw_ln_b, mb_conv, mb_conv_b, mb_dt_bias, mb_A_log, mb_D, mb_norm):
    bsz, seq, _ = x.shape
    depth = w_in.shape[0]
    nsb, pad_rows, lp = _tiling(seq)
    rows = nsb * SBR
    nc = nsb * SB_CHUNKS
    meta = jnp.broadcast_to(meta_tokens.astype(x.dtype)[None], (bsz, N_META, D_MODEL))
    h = jnp.concatenate([jnp.zeros((bsz, pad_rows, D_MODEL), x.dtype), meta, x], axis=1)

    vm = lambda r, c: pltpu.VMEM((r, c), F32)
    for l in range(depth):
        npre = norm_pre[l][None, :]
        wl = w_in[l]
        w_a = jnp.concatenate([wl[:, :4 * D_A], _pad_lanes(wl[:, 4 * D_A:4 * D_A + H_A]),
                               _pad_lanes(wl[:, 4 * D_A + H_A:P_A])], axis=1).astype(BF16)
        hp_a = jnp.stack([_pad_lanes(dn_A_log[l]), _pad_lanes(dn_dt_bias[l])])
        hp_a = jnp.pad(hp_a, ((0, 6), (0, 0)))
        o_a = _mixer_call(
            functools.partial(_deltanet_kernel, nsb, pad_rows), rows, bsz, lp, D_A,
            [npre, w_a, dn_conv[l], hp_a, dn_norm[l][None, :]],
            [vm(rows + CARRY, 3 * D_A), vm(rows, D_A), vm(rows, D_A), vm(rows, D_A),
             vm(rows, D_A), vm(rows, 2 * LANES), vm(rows, LANES), vm(rows, D_A), vm(rows, D_A),
             pltpu.VMEM((H_A, DH_A, DH_A), F32)],
            "deltanet", h)
        wb_l = wl[:, P_A:P_A + P_B]
        w_b = jnp.concatenate([wb_l[:, :4 * D_B], _pad_lanes(wb_l[:, 4 * D_B:4 * D_B + R_W]),
                               _pad_lanes(wb_l[:, 4 * D_B + R_W:])], axis=1).astype(BF16)
        mu = rw_mu[l]
        mu_b = jnp.concatenate([mu[:4 * D_B], _pad_lanes(mu[4 * D_B:4 * D_B + R_W]),
                                _pad_lanes(mu[4 * D_B + R_W:])])[None, :]
        w2p = jnp.pad(rw_w2[l], ((0, LANES - R_W), (0, 0))).astype(BF16)
        a2p = jnp.pad(rw_a2[l], ((0, LANES - R_A), (0, 0))).astype(BF16)
        vp_b = jnp.stack([rw_w0[l], rw_a0[l], rw_k_k[l], rw_k_a[l], rw_r_k[l].reshape(D_B),
                          rw_ln_w[l], rw_ln_b[l], jnp.zeros((D_B,), F32)])
        o_b = _mixer_call(
            functools.partial(_rwkv_kernel, nsb, pad_rows), rows, bsz, lp, D_B,
            [npre, w_b, mu_b, w2p, a2p, vp_b],
            [vm(rows + CARRY, WB)] + [vm(rows, D_B)] * 11
            + [pltpu.VMEM((H_B // 2, LANES, LANES), F32)],
            "rwkv7", h)
        wc_l = wl[:, P_A + P_B:]
        w_c = jnp.concatenate([
            _group_pad(wc_l[:, :D_C]), _group_pad(wc_l[:, D_C:2 * D_C]),
            wc_l[:, 2 * D_C:2 * D_C + 2 * G_C * N_C], _pad_lanes(wc_l[:, 2 * D_C + 2 * G_C * N_C:]),
        ], axis=1).astype(BF16)
        cw_c = jnp.concatenate([_group_pad(mb_conv[l][:, :D_C]), mb_conv[l][:, D_C:]], axis=1)
        cb_c = jnp.concatenate([_group_pad(mb_conv_b[l][:D_C]), mb_conv_b[l][D_C:]])[None, :]
        hp_c = jnp.pad(jnp.stack([_pad_lanes(mb_dt_bias[l]), _pad_lanes(mb_A_log[l])]),
                       ((0, 6), (0, 0)))
        vp_c = jnp.pad(jnp.stack([_group_pad(jnp.repeat(mb_D[l], DH_C)), _group_pad(mb_norm[l])]),
                       ((0, 6), (0, 0)))
        o_c = _mixer_call(
            functools.partial(_mamba_kernel, nc, pad_rows), rows, bsz, lp, D_CP,
            [npre, w_c, cw_c, cb_c, hp_c, vp_c],
            [vm(rows + CARRY, WC_CONV), vm(rows, D_CP), vm(rows, D_CP), vm(rows, G_C * N_C),
             vm(rows, G_C * N_C), vm(rows, LANES), vm(rows, LANES), vm(rows, D_CP),
             pltpu.VMEM((G_C, GW, N_C), F32)],
            "mamba2", h)
        wo = w_out[l]
        wo_a = wo[:D_A].astype(BF16)
        wo_b = wo[D_A:D_A + D_B].astype(BF16)
        wo_c = jnp.pad(wo[D_A + D_B:].reshape(G_C, HPG * DH_C, D_MODEL),
                       ((0, 0), (0, GW - HPG * DH_C), (0, 0))).reshape(D_CP, D_MODEL).astype(BF16)
        row_spec = lambda w: pl.BlockSpec((1, rows, w), lambda b, t: (b, t, 0))
        h = pl.pallas_call(
            _out_kernel,
            out_shape=jax.ShapeDtypeStruct((bsz, lp, D_MODEL), F32),
            grid=(bsz, lp // rows),
            in_specs=[row_spec(D_MODEL), row_spec(D_A), row_spec(D_B), row_spec(D_CP),
                      _const_spec(wo_a.shape), _const_spec(wo_b.shape), _const_spec(wo_c.shape),
                      _const_spec((1, D_MODEL))],
            out_specs=row_spec(D_MODEL),
            compiler_params=pltpu.CompilerParams(
                dimension_semantics=("arbitrary", "arbitrary"), vmem_limit_bytes=VMEM_LIMIT),
            name="out_proj",
        )(h, o_a, o_b, o_c, wo_a, wo_b, wo_c, norm_post[l][None, :])
    return h[:, pad_rows + N_META:]
```

```python
import functools

import jax
import jax.numpy as jnp
from jax import lax
from jax.experimental import pallas as pl
from jax.experimental.pallas import tpu as pltpu

F32 = jnp.float32
BF16 = jnp.bfloat16

D_MODEL = 1024
CHUNK = 64
SB_CHUNKS = 4
SBR = SB_CHUNKS * CHUNK
N_META = 16
CONV_K = 4
NORM_EPS = 1e-6
LANES = 128
CARRY = 8

H_A, DH_A = 6, 128
D_A = H_A * DH_A
H_B, DH_B = 8, 64
D_B = H_B * DH_B
R_W = R_A = 64
GN_EPS_B = 64e-5
H_C, DH_C = 12, 64
D_C = H_C * DH_C
G_C, N_C = 4, 128
HPG = H_C // G_C
GW = 256
D_CP = G_C * GW
P_A = 4 * D_A + 2 * H_A
P_B = 4 * D_B + R_W + R_A
P_C = 2 * D_C + 2 * G_C * N_C + H_C

WA = 4 * D_A + 2 * LANES
WB = 4 * D_B + 2 * LANES
WC_CONV = D_CP + 2 * G_C * N_C
WC = D_CP + WC_CONV + LANES

VMEM_LIMIT = 56 * 1024 * 1024
MAX_TILE_SB = 3
PW = 256
RWKV_GROUP = 4
UNROLL_SB = True
UNROLL_CHUNKS = True


def _dot(a, b):
    return jnp.dot(a.astype(BF16), b.astype(BF16), preferred_element_type=F32)


def _dot_nt(a, b):
    return lax.dot_general(a.astype(BF16), b.astype(BF16), (((1,), (1,)), ((), ())),
                           preferred_element_type=F32)


def _dot_tn(a, b):
    return lax.dot_general(a.astype(BF16), b.astype(BF16), (((0,), (0,)), ((), ())),
                           preferred_element_type=F32)


def _split2(x):
    hi = x.astype(BF16)
    lo = (x - hi.astype(F32)).astype(BF16)
    return hi, lo


def _split3(x):
    hi = x.astype(BF16)
    r = x - hi.astype(F32)
    mid = r.astype(BF16)
    lo = (r - mid.astype(F32)).astype(BF16)
    return hi, mid, lo


def _dot_lx3(l_bf16, x):
    hi, mid, lo = _split3(x)
    d = lambda p: jnp.dot(l_bf16, p, preferred_element_type=F32)
    return d(hi) + d(mid) + d(lo)


def _dot_x2r(x, r_bf16):
    hi, lo = _split2(x)
    d = lambda p: jnp.dot(p, r_bf16, preferred_element_type=F32)
    return d(hi) + d(lo)


def _dot_x3r(x, r_bf16):
    hi, mid, lo = _split3(x)
    d = lambda p: jnp.dot(p, r_bf16, preferred_element_type=F32)
    return d(hi) + d(mid) + d(lo)


def _to_wide(m, lane_blk):
    s = [m[c * CHUNK:(c + 1) * CHUNK, :] for c in range(SB_CHUNKS)]
    out = s[SB_CHUNKS - 1]
    for c in range(SB_CHUNKS - 2, -1, -1):
        out = jnp.where(lane_blk == c, s[c], out)
    return out


def _to_block_diag(w, lane_blk):
    wb = w.astype(BF16)
    zero = jnp.zeros_like(wb)
    return jnp.concatenate([jnp.where(lane_blk == c, wb, zero) for c in range(SB_CHUNKS)], axis=0)


def _neumann_inverses(xs, eye_w, lane_blk):
    bd = lambda w: _to_block_diag(w, lane_blk)
    mm = lambda a, b: jnp.dot(a.astype(BF16), b, preferred_element_type=F32)
    xw = [_to_wide(x, lane_blk) for x in xs]
    p = [eye_w + w for w in xw]
    xk = [mm(w, bd(w)) for w in xw]
    for _ in range(4):
        both = [mm(a, jnp.concatenate([bd(a), bd(b)], axis=1)) for a, b in zip(xk, p)]
        xk = [t[:, 0:SBR] for t in both]
        p = [b + t[:, SBR:2 * SBR] for b, t in zip(p, both)]
    p = [b + mm(a, bd(b)) for a, b in zip(xk, p)]
    return [bd(b) for b in p]


def _rows(i, n):
    if isinstance(i, int):
        return slice(i * n, (i + 1) * n)
    return pl.ds(pl.multiple_of(i * n, n), n)


def _row(i, n):
    if isinstance(i, int):
        return slice(i * n, i * n + 1)
    return pl.ds(pl.multiple_of(i * n, n), 1)


def _repeat(n, body, unroll):
    if unroll:
        for i in range(n):
            body(i, 0)
    else:
        lax.fori_loop(0, n, body, 0)


def _silu(x):
    return x * jax.nn.sigmoid(x)


def _softplus(x):
    return jnp.maximum(x, 0.0) + jnp.log1p(jnp.exp(-jnp.abs(x)))


def _iota2(shape, dim):
    return lax.broadcasted_iota(jnp.int32, shape, dim)


def _tri_masks(n):
    r = _iota2((n, n), 0)
    c = _iota2((n, n), 1)
    same = (r // CHUNK) == (c // CHUNK)
    incl = same & (r >= c)
    strict = same & (r > c)
    tri = jnp.where(incl, 1.0, 0.0).astype(BF16)
    blk = jnp.where(same, 1.0, 0.0).astype(BF16)
    return incl, strict, tri, blk


def _wide_consts():
    r = _iota2((CHUNK, SBR), 0)
    c = _iota2((CHUNK, SBR), 1)
    return jnp.where(c % CHUNK == r, 1.0, 0.0).astype(F32), c // CHUNK


def _normed_input(h_ref, npre_ref, t_idx, rows, pad_rows):
    h = h_ref[0]
    ms = jnp.mean(h * h, axis=-1, keepdims=True)
    hn = h * lax.rsqrt(ms + NORM_EPS) * npre_ref[...]
    gid = t_idx * rows + _iota2((rows, 1), 0)
    valid = gid >= pad_rows
    hn = jnp.where(valid, hn, 0.0)
    return hn.astype(BF16), valid


def _chunk_last(x):
    return jnp.concatenate(
        [jnp.broadcast_to(x[(c + 1) * CHUNK - 1:(c + 1) * CHUNK, :], (CHUNK, x.shape[1]))
         for c in range(SB_CHUNKS)], axis=0)


def _pipeline(n, produce, consume):
    produce(0)
    for j in range(n):
        if j + 1 < n:
            produce(j + 1)
        consume(j)


def _causal_conv_block(ubuf, cw_ref, lo, rows):
    full = ubuf[0:CARRY + rows, lo:lo + LANES]
    tap = lambda j: cw_ref[j:j + 1, lo:lo + LANES]
    prev = pltpu.roll(full, 1, axis=0)
    near = full * tap(3) + prev * tap(2)
    far = full * tap(1) + prev * tap(0)
    return (near + pltpu.roll(far, 2, axis=0))[CARRY:]


def _deltanet_kernel(nsb, pad_rows, h_ref, npre_ref, w_ref, cw_ref, hp_ref, nw_ref, o_ref,
                     hn_s, ubuf, q_s, k_s, v_s, z_s, gb_s, gl_s, u_s, w_s, st_ref):
    rows = nsb * SBR
    t_idx = pl.program_id(1)

    @pl.when(t_idx == 0)
    def _():
        ubuf[0:CARRY, :] = jnp.zeros((CARRY, 3 * D_A), F32)
        st_ref[...] = jnp.zeros(st_ref.shape, F32)

    hn, valid = _normed_input(h_ref, npre_ref, t_idx, rows, pad_rows)
    hn_s[...] = hn
    n_conv = 3 * D_A // PW
    n_z = D_A // PW

    def project(j):
        cols = slice(j * PW, (j + 1) * PW) if j <= n_conv + n_z - 1 else slice(4 * D_A, WA)
        p = jnp.dot(hn_s[...], w_ref[:, cols], preferred_element_type=F32)
        if j < n_conv:
            ubuf[CARRY:CARRY + rows, cols] = p
        elif j < n_conv + n_z:
            z_s[:, (j - n_conv) * PW:(j - n_conv + 1) * PW] = p
        else:
            g = -jnp.exp(hp_ref[0:1, :]) * _softplus(p[:, 0:LANES] + hp_ref[1:2, :])
            gb_s[:, 0:LANES] = jnp.where(valid, g, 0.0)
            gb_s[:, LANES:2 * LANES] = jnp.where(valid, jax.nn.sigmoid(p[:, LANES:2 * LANES]), 0.0)

    def activate(j):
        if j >= n_conv:
            return
        for blk_i in range(j * PW // LANES, (j + 1) * PW // LANES):
            y = _silu(_causal_conv_block(ubuf, cw_ref, blk_i * LANES, rows))
            if blk_i < 2 * H_A:
                y = y * lax.rsqrt(jnp.sum(y * y, axis=-1, keepdims=True) + 1e-6)
            dst = (q_s, k_s, v_s)[blk_i // H_A]
            hl = (blk_i % H_A) * LANES
            dst[:, hl:hl + LANES] = y

    _pipeline(n_conv + n_z + 1, project, activate)
    ubuf[0:CARRY, :] = ubuf[rows:rows + CARRY, :]

    incl, strict, tri, blk = _tri_masks(SBR)
    eye_w, lane_blk = _wide_consts()
    scale = DH_A ** -0.5

    def sb_body(sb, carry):
        rs = _rows(sb, SBR)
        g_sb = gb_s[rs, 0:LANES]
        gc = _dot_lx3(tri, g_sb)
        gl = _chunk_last(gc)
        gl_s[rs, :] = gl
        gct = gc.T
        beta_sb = gb_s[rs, LANES:2 * LANES]
        heads = range(H_A)
        hs = [slice(hd * DH_A, (hd + 1) * DH_A) for hd in heads]
        k = [k_s[rs, hs[hd]] for hd in heads]
        gcol = [gc[:, hd:hd + 1] for hd in heads]
        bcol = [beta_sb[:, hd:hd + 1] for hd in heads]
        dmask = [jnp.where(incl, jnp.exp(jnp.where(incl, gcol[hd] - gct[hd:hd + 1, :], 0.0)), 0.0)
                 for hd in heads]
        kb = [k[hd] * bcol[hd] for hd in heads]
        a_neg = [jnp.where(strict, -(_dot_nt(kb[hd], k[hd]) * dmask[hd]), 0.0) for hd in heads]
        t_inv = _neumann_inverses(a_neg, eye_w, lane_blk)
        eg = [jnp.exp(gcol[hd]) for hd in heads]
        sol = [_dot(t_inv[hd], jnp.concatenate([v_s[rs, hs[hd]] * bcol[hd], kb[hd] * eg[hd]], axis=1))
               for hd in heads]
        qs = [q_s[rs, hs[hd]] * scale for hd in heads]
        attn = [_dot_nt(qs[hd], k[hd]) * dmask[hd] for hd in heads]
        asol = [_dot(attn[hd], sol[hd]) for hd in heads]
        for hd in heads:
            u_s[rs, hs[hd]] = sol[hd][:, 0:DH_A]
            w_s[rs, hs[hd]] = sol[hd][:, DH_A:2 * DH_A]
            v_s[rs, hs[hd]] = asol[hd][:, 0:DH_A]
            q_s[rs, hs[hd]] = qs[hd] * eg[hd] - asol[hd][:, DH_A:2 * DH_A]
            k_s[rs, hs[hd]] = k[hd] * jnp.exp(gl[:, hd:hd + 1] - gcol[hd])
        return carry

    _repeat(nsb, sb_body, UNROLL_SB)

    def chunk_body(c, carry):
        rs = _rows(c, CHUNK)
        heads = range(H_A)
        hs = [slice(hd * DH_A, (hd + 1) * DH_A) for hd in heads]
        state = [st_ref[hd] for hd in heads]
        res = [_dot(jnp.concatenate([w_s[rs, hs[hd]], q_s[rs, hs[hd]]], axis=0), state[hd])
               for hd in heads]
        v_new = [u_s[rs, hs[hd]] - res[hd][0:CHUNK] for hd in heads]
        upd = [_dot_tn(k_s[rs, hs[hd]], v_new[hd]) for hd in heads]
        for hd in heads:
            v_s[rs, hs[hd]] = v_s[rs, hs[hd]] + res[hd][CHUNK:2 * CHUNK]
            g_tot = jnp.exp(gl_s[_row(c, CHUNK), hd:hd + 1])
            st_ref[hd] = state[hd] * g_tot + upd[hd]
        return carry

    _repeat(nsb * SB_CHUNKS, chunk_body, UNROLL_CHUNKS)

    nw = nw_ref[...]
    for hd in range(H_A):
        hs = slice(hd * DH_A, (hd + 1) * DH_A)
        o = v_s[:, hs]
        o = o * lax.rsqrt(jnp.mean(o * o, axis=-1, keepdims=True) + NORM_EPS) * nw
        o_ref[0, :, hs] = o * _silu(z_s[:, hs])


def _seg_sum(x, seg):
    outs = []
    for b in range(x.shape[1] // LANES):
        outs.append(_dot_x2r(x[:, b * LANES:(b + 1) * LANES], seg))
    return jnp.concatenate(outs, axis=1)


def _rwkv_kernel(nsb, pad_rows, h_ref, npre_ref, w_ref, mu_ref, w2_ref, a2_ref, vp_ref, o_ref,
                 hn_s, pbuf, r_s, k_s, v_s, a_s, b_s, lw_s, g_s, y_s, u0_s, be_s, ke_s, st_ref):
    rows = nsb * SBR
    t_idx = pl.program_id(1)

    @pl.when(t_idx == 0)
    def _():
        pbuf[0:CARRY, :] = jnp.zeros((CARRY, WB), F32)
        st_ref[...] = jnp.zeros(st_ref.shape, F32)

    hn, _ = _normed_input(h_ref, npre_ref, t_idx, rows, pad_rows)
    hn_s[...] = hn

    def mixed(lo, width):
        full = pbuf[0:CARRY + rows, lo:lo + width]
        cur = full[CARRY:]
        prev = pltpu.roll(full, 1, axis=0)[CARRY:]
        return cur + (prev - cur) * mu_ref[:, lo:lo + width]

    w0 = vp_ref[0:1, :]
    a0 = vp_ref[1:2, :]
    k_k = vp_ref[2:3, :]
    k_a = vp_ref[3:4, :]
    r_k = vp_ref[4:5, :]
    ln_w = vp_ref[5:6, :]
    ln_b = vp_ref[6:7, :]

    hr = _iota2((LANES, LANES), 0) // DH_B
    hc = _iota2((LANES, LANES), 1) // DH_B
    bd = hr == hc
    seg = jnp.where(bd, 1.0, 0.0).astype(BF16)

    col_lo = (4 * D_B, D_B, 0, 2 * D_B, 3 * D_B)
    col_w = (2 * LANES, D_B, D_B, D_B, D_B)

    def project(j):
        cols = slice(col_lo[j], col_lo[j] + col_w[j])
        pbuf[CARRY:CARRY + rows, cols] = jnp.dot(hn_s[...], w_ref[:, cols],
                                                 preferred_element_type=F32)

    def activate(j):
        if j == 0:
            w_lo = mixed(4 * D_B, LANES)
            a_lo = mixed(4 * D_B + LANES, LANES)
            lw_s[...] = -jnp.exp(-_softplus(-(w0 + _dot(jnp.tanh(w_lo), w2_ref[...]))) - 0.5)
            u0_s[...] = jax.nn.sigmoid(a0 + _dot(a_lo, a2_ref[...]))
        elif j == 1:
            k = mixed(D_B, D_B)
            a_lr = u0_s[...]
            kk = k * k_k
            kk = kk * lax.rsqrt(_seg_sum(kk * kk, seg) + 1e-6)
            k_s[...] = k * (1.0 + (a_lr - 1.0) * k_a)
            a_s[...] = -kk
            b_s[...] = kk * a_lr
        elif j == 2:
            r_s[...] = mixed(0, D_B)
        elif j == 3:
            v_s[...] = mixed(2 * D_B, D_B)
        else:
            g_s[...] = _silu(mixed(3 * D_B, D_B))

    _pipeline(len(col_lo), project, activate)
    pbuf[0:CARRY, :] = pbuf[rows:rows + CARRY, :]

    incl, strict, tri, blk = _tri_masks(SBR)
    eye_w, lane_blk = _wide_consts()
    lane = _iota2((SBR, LANES), 1)
    half = (lane < DH_B, lane >= DH_B)

    def sb_body(sb, carry):
        rs = _rows(sb, SBR)
        lw = lw_s[rs, :]
        cl = _dot_lx3(tri, lw)
        cl_end = _chunk_last(cl)
        e_neg = jnp.exp(-cl)
        e_end = jnp.exp(cl_end - cl)
        a_t = a_s[rs, :] * jnp.exp(cl - lw)
        r_t = r_s[rs, :] * jnp.exp(cl)
        b_c = b_s[rs, :]
        k_c = k_s[rs, :]
        v_c = v_s[rs, :]
        b_t = b_c * e_neg
        k_t = k_c * e_neg
        be_s[rs, :] = b_c * e_end
        ke_s[rs, :] = k_c * e_end
        lw_s[rs, :] = jnp.exp(cl_end)
        for first in range(0, H_B, RWKV_GROUP):
            rwkv_heads(rs, range(first, first + RWKV_GROUP), a_t, r_t, b_t, k_t, v_c)
        return carry

    def rwkv_heads(rs, heads, a_t, r_t, b_t, k_t, v_c):
        ls = {hd: slice((hd // 2) * LANES, (hd // 2 + 1) * LANES) for hd in heads}
        am = {hd: jnp.where(half[hd % 2], a_t[:, ls[hd]], 0.0) for hd in heads}
        rm = {hd: jnp.where(half[hd % 2], r_t[:, ls[hd]], 0.0) for hd in heads}
        quad = {hd: _dot_nt(jnp.concatenate([am[hd], rm[hd]], axis=0),
                            jnp.concatenate([b_t[:, ls[hd]], k_t[:, ls[hd]]], axis=0))
                for hd in heads}
        ab = [jnp.where(strict, quad[hd][0:SBR, 0:SBR], 0.0) for hd in heads]
        t_inv = dict(zip(heads, _neumann_inverses(ab, eye_w, lane_blk)))
        akv = {hd: _dot(jnp.where(strict, quad[hd][0:SBR, SBR:2 * SBR], 0.0), v_c[:, ls[hd]])
               for hd in heads}
        sol = {hd: _dot(t_inv[hd], jnp.concatenate([am[hd], akv[hd]], axis=1)) for hd in heads}
        rbsol = {hd: _dot(jnp.where(incl, quad[hd][SBR:2 * SBR, 0:SBR], 0.0), sol[hd])
                 for hd in heads}
        rkv = {hd: _dot(jnp.where(incl, quad[hd][SBR:2 * SBR, SBR:2 * SBR], 0.0), v_c[:, ls[hd]])
               for hd in heads}
        for h0 in list(heads)[0::2]:
            h1 = h0 + 1
            a_s[rs, ls[h0]] = sol[h0][:, 0:LANES] + sol[h1][:, 0:LANES]
            b_s[rs, ls[h0]] = rm[h0] + rbsol[h0][:, 0:LANES] + rm[h1] + rbsol[h1][:, 0:LANES]
            u0_s[rs, ls[h0]] = jnp.where(half[0], sol[h0][:, LANES:2 * LANES],
                                         sol[h1][:, LANES:2 * LANES])
            y_s[rs, ls[h0]] = jnp.where(half[0], rbsol[h0][:, LANES:2 * LANES] + rkv[h0],
                                        rbsol[h1][:, LANES:2 * LANES] + rkv[h1])

    _repeat(nsb, sb_body, UNROLL_SB)

    def chunk_body(c, carry):
        rs = _rows(c, CHUNK)
        pairs = range(H_B // 2)
        ls = [slice(pr * LANES, (pr + 1) * LANES) for pr in pairs]
        state = [st_ref[pr] for pr in pairs]
        res = [_dot_nt(jnp.concatenate([a_s[rs, ls[pr]], b_s[rs, ls[pr]]], axis=0), state[pr])
               for pr in pairs]
        u = [res[pr][0:CHUNK] + u0_s[rs, ls[pr]] for pr in pairs]
        upd = [_dot_tn(jnp.concatenate([u[pr], v_s[rs, ls[pr]]], axis=0),
                       jnp.concatenate([be_s[rs, ls[pr]], ke_s[rs, ls[pr]]], axis=0))
               for pr in pairs]
        for pr in pairs:
            y_s[rs, ls[pr]] = y_s[rs, ls[pr]] + res[pr][CHUNK:2 * CHUNK]
            gam = lw_s[_row(c, CHUNK), ls[pr]]
            st_ref[pr] = jnp.where(bd, state[pr] * gam + upd[pr], 0.0)
        return carry

    _repeat(nsb * SB_CHUNKS, chunk_body, UNROLL_CHUNKS)

    y = y_s[...]
    r = r_s[...]
    k2 = k_s[...]
    v = v_s[...]
    mean = _seg_sum(y, seg) * (1.0 / DH_B)
    yc = y - mean
    var = _seg_sum(yc * yc, seg) * (1.0 / DH_B)
    yn = yc * lax.rsqrt(var + GN_EPS_B) * ln_w + ln_b
    bonus = _seg_sum(r * k2 * r_k, seg) * v
    o_ref[0] = (yn + bonus) * g_s[...]


def _mamba_kernel(nc, pad_rows, h_ref, npre_ref, w_ref, cw_ref, cb_ref, hp_ref, vp_ref, o_ref,
                  hn_s, ubuf, z_s, x_s, bm_s, cm_s, dt_s, a_s, e_in_s, st_ref):
    rows = nc * CHUNK
    t_idx = pl.program_id(1)

    @pl.when(t_idx == 0)
    def _():
        ubuf[0:CARRY, :] = jnp.zeros((CARRY, WC_CONV), F32)
        st_ref[...] = jnp.zeros(st_ref.shape, F32)

    hn, valid = _normed_input(h_ref, npre_ref, t_idx, rows, pad_rows)
    hn_s[...] = hn
    n_conv = WC_CONV // PW
    n_z = D_CP // PW

    def project(j):
        if j < n_conv:
            cols = slice(D_CP + j * PW, D_CP + (j + 1) * PW)
            ubuf[CARRY:CARRY + rows, j * PW:(j + 1) * PW] = jnp.dot(
                hn_s[...], w_ref[:, cols], preferred_element_type=F32)
        elif j < n_conv + n_z:
            cols = slice((j - n_conv) * PW, (j - n_conv + 1) * PW)
            z_s[:, cols] = jnp.dot(hn_s[...], w_ref[:, cols], preferred_element_type=F32)
        else:
            dtp = jnp.dot(hn_s[...], w_ref[:, D_CP + WC_CONV:WC], preferred_element_type=F32)
            dt = jnp.where(valid, _softplus(dtp + hp_ref[0:1, :]), 0.0)
            dt_s[...] = dt
            a_s[...] = -jnp.exp(hp_ref[1:2, :]) * dt

    def activate(j):
        if j >= n_conv:
            return
        for blk_i in range(j * PW // LANES, (j + 1) * PW // LANES):
            lo = blk_i * LANES
            y = _silu(_causal_conv_block(ubuf, cw_ref, lo, rows) + cb_ref[:, lo:lo + LANES])
            if lo < D_CP:
                x_s[:, lo:lo + LANES] = y
            elif lo < D_CP + G_C * N_C:
                bm_s[:, lo - D_CP:lo - D_CP + LANES] = jnp.where(valid, y, 0.0)
            else:
                o2 = lo - D_CP - G_C * N_C
                cm_s[:, o2:o2 + LANES] = jnp.where(valid, y, 0.0)

    _pipeline(n_conv + n_z + 1, project, activate)
    ubuf[0:CARRY, :] = ubuf[rows:rows + CARRY, :]

    incl, _, tri, blk = _tri_masks(SBR)
    glane = _iota2((SBR, GW), 1)
    lane = _iota2((SBR, LANES), 1)
    prow = _iota2((GW, N_C), 0)
    d_skip = vp_ref[0:1, :]

    def expand_heads(f):
        outs = []
        for b in range(D_CP // LANES):
            gi, hb = divmod(b, GW // LANES)
            h0 = gi * HPG + 2 * hb
            second = f[:, h0 + 1:h0 + 2] if hb == 0 else 0.0
            outs.append(jnp.where(lane < DH_C, f[:, h0:h0 + 1], second))
        return jnp.concatenate(outs, axis=1)

    def sb_body(sb, carry):
        rs = _rows(sb, SBR)
        a_sb = a_s[rs, :]
        acs = _dot_lx3(tri, a_sb)
        a_end = _chunk_last(acs)
        dt_sb = dt_s[rs, :]
        acst = acs.T
        dtt = dt_sb.T
        e_in_s[rs, :] = expand_heads(jnp.exp(acs))
        a_s[rs, :] = jnp.exp(a_end)
        xs = x_s[rs, :]
        x_s[rs, :] = xs * expand_heads(dt_sb * jnp.exp(a_end - acs))
        groups = range(G_C)
        gs = [slice(gi * GW, (gi + 1) * GW) for gi in groups]
        cb = [_dot_nt(cm_s[rs, gi * N_C:(gi + 1) * N_C], bm_s[rs, gi * N_C:(gi + 1) * N_C])
              for gi in groups]
        m = []
        for hd in range(H_C):
            decay = jnp.exp(jnp.where(incl, acs[:, hd:hd + 1] - acst[hd:hd + 1, :], 0.0))
            m.append(jnp.where(incl, cb[hd // HPG] * decay, 0.0) * dtt[hd:hd + 1, :])
        in_head = [(glane >= hh * DH_C) & (glane < (hh + 1) * DH_C) for hh in range(HPG)]
        terms = [_dot(m[hd], jnp.where(in_head[hd % HPG], xs[:, gs[hd // HPG]], 0.0))
                 for hd in range(H_C)]
        for gi in groups:
            y = d_skip[:, gs[gi]] * xs[:, gs[gi]]
            for hh in range(HPG):
                y = y + terms[gi * HPG + hh]
            o_ref[0, rs, gs[gi]] = y
        return carry

    _repeat(nc // SB_CHUNKS, sb_body, UNROLL_SB)

    def chunk_body(c, carry):
        rs = _rows(c, CHUNK)
        groups = range(G_C)
        gs = [slice(gi * GW, (gi + 1) * GW) for gi in groups]
        ns = [slice(gi * N_C, (gi + 1) * N_C) for gi in groups]
        state = [st_ref[gi] for gi in groups]
        y_off = [_dot_nt(cm_s[rs, ns[gi]], state[gi]) for gi in groups]
        upd = [_dot_tn(x_s[rs, gs[gi]], bm_s[rs, ns[gi]]) for gi in groups]
        for gi in groups:
            o_ref[0, rs, gs[gi]] = o_ref[0, rs, gs[gi]] + y_off[gi] * e_in_s[rs, gs[gi]]
            scale_rows = jnp.zeros((GW, N_C), F32)
            for hh in range(HPG):
                hd = gi * HPG + hh
                scale_rows = jnp.where((prow >= hh * DH_C) & (prow < (hh + 1) * DH_C),
                                       a_s[_row(c, CHUNK), hd:hd + 1], scale_rows)
            st_ref[gi] = state[gi] * scale_rows + upd[gi]
        return carry

    _repeat(nc, chunk_body, UNROLL_CHUNKS)

    nw = vp_ref[1:2, :]
    for gi in range(G_C):
        gs = slice(gi * GW, (gi + 1) * GW)
        y = o_ref[0, :, gs] * _silu(z_s[:, gs])
        ms = jnp.sum(y * y, axis=-1, keepdims=True) * (1.0 / (HPG * DH_C))
        o_ref[0, :, gs] = y * lax.rsqrt(ms + NORM_EPS) * nw[:, gs]


def _out_kernel(h_ref, oa_ref, ob_ref, oc_ref, wa_ref, wb_ref, wc_ref, npost_ref, o_ref):
    out = (_dot(oa_ref[0], wa_ref[...]) + _dot(ob_ref[0], wb_ref[...])
           + _dot(oc_ref[0], wc_ref[...]))
    ms = jnp.mean(out * out, axis=-1, keepdims=True)
    o_ref[0] = h_ref[0] + out * lax.rsqrt(ms + NORM_EPS) * npost_ref[...]


def _pad_lanes(x, width=LANES):
    return jnp.pad(x, [(0, 0)] * (x.ndim - 1) + [(0, width - x.shape[-1])])


def _group_pad(x):
    lead = x.shape[:-1]
    xg = x.reshape(lead + (G_C, HPG * DH_C))
    xg = jnp.pad(xg, [(0, 0)] * len(lead) + [(0, 0), (0, GW - HPG * DH_C)])
    return xg.reshape(lead + (D_CP,))


def _tiling(seq):
    n_sb = -(-(N_META + seq) // SBR)
    best = None
    for nsb in range(MAX_TILE_SB, 0, -1):
        total = n_sb + (-n_sb) % nsb
        if best is None or total < best[1]:
            best = (nsb, total)
    nsb, total = best
    lp = total * SBR
    return nsb, lp - N_META - seq, lp


def _const_spec(shape):
    return pl.BlockSpec(shape, lambda b, t: (0,) * len(shape))


def _mixer_call(body, rows, bsz, lp, out_w, consts, scratch, name, h):
    tile = pl.BlockSpec((1, rows, D_MODEL), lambda b, t: (b, t, 0))
    return pl.pallas_call(
        body,
        out_shape=jax.ShapeDtypeStruct((bsz, lp, out_w), F32),
        grid=(bsz, lp // rows),
        in_specs=[tile] + [_const_spec(c.shape) for c in consts],
        out_specs=pl.BlockSpec((1, rows, out_w), lambda b, t: (b, t, 0)),
        scratch_shapes=scratch,
        compiler_params=pltpu.CompilerParams(
            dimension_semantics=("arbitrary", "arbitrary"), vmem_limit_bytes=VMEM_LIMIT),
        name=name,
    )(h, *consts)


def kernel(x, meta_tokens, norm_pre, norm_post, w_in, w_out, dn_conv, dn_A_log, dn_dt_bias, dn_norm, rw_mu, rw_w0, rw_w2, rw_a0, rw_a2, rw_k_k, rw_k_a, rw_r_k, rw_ln_w, rw_ln_b, mb_conv, mb_conv_b, mb_dt_bias, mb_A_log, mb_D, mb_norm):
    bsz, seq, _ = x.shape
    depth = w_in.shape[0]
    nsb, pad_rows, lp = _tiling(seq)
    rows = nsb * SBR
    nc = nsb * SB_CHUNKS
    meta = jnp.broadcast_to(meta_tokens.astype(x.dtype)[None], (bsz, N_META, D_MODEL))
    h = jnp.concatenate([jnp.zeros((bsz, pad_rows, D_MODEL), x.dtype), meta, x], axis=1)

    vm = lambda r, c: pltpu.VMEM((r, c), F32)
    hn_scratch = pltpu.VMEM((rows, D_MODEL), BF16)
    for l in range(depth):
        npre = norm_pre[l][None, :]
        wl = w_in[l]
        w_a = jnp.concatenate([wl[:, :4 * D_A], _pad_lanes(wl[:, 4 * D_A:4 * D_A + H_A]),
                               _pad_lanes(wl[:, 4 * D_A + H_A:P_A])], axis=1).astype(BF16)
        hp_a = jnp.stack([_pad_lanes(dn_A_log[l]), _pad_lanes(dn_dt_bias[l])])
        hp_a = jnp.pad(hp_a, ((0, 6), (0, 0)))
        o_a = _mixer_call(
            functools.partial(_deltanet_kernel, nsb, pad_rows), rows, bsz, lp, D_A,
            [npre, w_a, dn_conv[l], hp_a, dn_norm[l][None, :]],
            [hn_scratch, vm(rows + CARRY, 3 * D_A), vm(rows, D_A), vm(rows, D_A), vm(rows, D_A),
             vm(rows, D_A), vm(rows, 2 * LANES), vm(rows, LANES), vm(rows, D_A), vm(rows, D_A),
             pltpu.VMEM((H_A, DH_A, DH_A), F32)],
            "deltanet", h)
        wb_l = wl[:, P_A:P_A + P_B]
        w_b = jnp.concatenate([wb_l[:, :4 * D_B], _pad_lanes(wb_l[:, 4 * D_B:4 * D_B + R_W]),
                               _pad_lanes(wb_l[:, 4 * D_B + R_W:])], axis=1).astype(BF16)
        mu = rw_mu[l]
        mu_b = jnp.concatenate([mu[:4 * D_B], _pad_lanes(mu[4 * D_B:4 * D_B + R_W]),
                                _pad_lanes(mu[4 * D_B + R_W:])])[None, :]
        w2p = jnp.pad(rw_w2[l], ((0, LANES - R_W), (0, 0))).astype(BF16)
        a2p = jnp.pad(rw_a2[l], ((0, LANES - R_A), (0, 0))).astype(BF16)
        vp_b = jnp.stack([rw_w0[l], rw_a0[l], rw_k_k[l], rw_k_a[l], rw_r_k[l].reshape(D_B),
                          rw_ln_w[l], rw_ln_b[l], jnp.zeros((D_B,), F32)])
        o_b = _mixer_call(
            functools.partial(_rwkv_kernel, nsb, pad_rows), rows, bsz, lp, D_B,
            [npre, w_b, mu_b, w2p, a2p, vp_b],
            [hn_scratch, vm(rows + CARRY, WB)] + [vm(rows, D_B)] * 11
            + [pltpu.VMEM((H_B // 2, LANES, LANES), F32)],
            "rwkv7", h)
        wc_l = wl[:, P_A + P_B:]
        w_c = jnp.concatenate([
            _group_pad(wc_l[:, :D_C]), _group_pad(wc_l[:, D_C:2 * D_C]),
            wc_l[:, 2 * D_C:2 * D_C + 2 * G_C * N_C], _pad_lanes(wc_l[:, 2 * D_C + 2 * G_C * N_C:]),
        ], axis=1).astype(BF16)
        cw_c = jnp.concatenate([_group_pad(mb_conv[l][:, :D_C]), mb_conv[l][:, D_C:]], axis=1)
        cb_c = jnp.concatenate([_group_pad(mb_conv_b[l][:D_C]), mb_conv_b[l][D_C:]])[None, :]
        hp_c = jnp.pad(jnp.stack([_pad_lanes(mb_dt_bias[l]), _pad_lanes(mb_A_log[l])]),
                       ((0, 6), (0, 0)))
        vp_c = jnp.pad(jnp.stack([_group_pad(jnp.repeat(mb_D[l], DH_C)), _group_pad(mb_norm[l])]),
                       ((0, 6), (0, 0)))
        o_c = _mixer_call(
            functools.partial(_mamba_kernel, nc, pad_rows), rows, bsz, lp, D_CP,
            [npre, w_c, cw_c, cb_c, hp_c, vp_c],
            [hn_scratch, vm(rows + CARRY, WC_CONV), vm(rows, D_CP), vm(rows, D_CP),
             vm(rows, G_C * N_C),
             vm(rows, G_C * N_C), vm(rows, LANES), vm(rows, LANES), vm(rows, D_CP),
             pltpu.VMEM((G_C, GW, N_C), F32)],
            "mamba2", h)
        wo = w_out[l]
        wo_a = wo[:D_A].astype(BF16)
        wo_b = wo[D_A:D_A + D_B].astype(BF16)
        wo_c = jnp.pad(wo[D_A + D_B:].reshape(G_C, HPG * DH_C, D_MODEL),
                       ((0, 0), (0, GW - HPG * DH_C), (0, 0))).reshape(D_CP, D_MODEL).astype(BF16)
        row_spec = lambda w: pl.BlockSpec((1, rows, w), lambda b, t: (b, t, 0))
        h = pl.pallas_call(
            _out_kernel,
            out_shape=jax.ShapeDtypeStruct((bsz, lp, D_MODEL), F32),
            grid=(bsz, lp // rows),
            in_specs=[row_spec(D_MODEL), row_spec(D_A), row_spec(D_B), row_spec(D_CP),
                      _const_spec(wo_a.shape), _const_spec(wo_b.shape), _const_spec(wo_c.shape),
                      _const_spec((1, D_MODEL))],
            out_specs=row_spec(D_MODEL),
            compiler_params=pltpu.CompilerParams(
                dimension_semantics=("arbitrary", "arbitrary"), vmem_limit_bytes=VMEM_LIMIT),
            name="out_proj",
        )(h, o_a, o_b, o_c, wo_a, wo_b, wo_c, norm_post[l][None, :])
    return h[:, pad_rows + N_META:]
```

```python
import functools

import jax
import jax.numpy as jnp
from jax import lax
from jax.experimental import pallas as pl
from jax.experimental.pallas import tpu as pltpu

F32 = jnp.float32
BF16 = jnp.bfloat16

D_MODEL = 1024
CHUNK = 64
SB_CHUNKS = 4
SBR = SB_CHUNKS * CHUNK
N_META = 16
CONV_K = 4
NORM_EPS = 1e-6
LANES = 128
CARRY = 8

H_A, DH_A = 6, 128
D_A = H_A * DH_A
H_B, DH_B = 8, 64
D_B = H_B * DH_B
R_W = R_A = 64
GN_EPS_B = 64e-5
H_C, DH_C = 12, 64
D_C = H_C * DH_C
G_C, N_C = 4, 128
HPG = H_C // G_C
GW = 256
D_CP = G_C * GW
P_A = 4 * D_A + 2 * H_A
P_B = 4 * D_B + R_W + R_A
P_C = 2 * D_C + 2 * G_C * N_C + H_C

WA = 4 * D_A + 2 * LANES
WB = 4 * D_B + 2 * LANES
WC_CONV = D_CP + 2 * G_C * N_C
WC = D_CP + WC_CONV + LANES

VMEM_LIMIT = 56 * 1024 * 1024
MAX_TILE_SB = 3
PW = 256
DELTA_GROUP = 3
RWKV_GROUP = 8


def _dot(a, b):
    return jnp.dot(a.astype(BF16), b.astype(BF16), preferred_element_type=F32)


def _dot_nt(a, b):
    return lax.dot_general(a.astype(BF16), b.astype(BF16), (((1,), (1,)), ((), ())),
                           preferred_element_type=F32)


def _dot_tn(a, b):
    return lax.dot_general(a.astype(BF16), b.astype(BF16), (((0,), (0,)), ((), ())),
                           preferred_element_type=F32)


def _split2(x):
    hi = x.astype(BF16)
    lo = (x - hi.astype(F32)).astype(BF16)
    return hi, lo


def _split3(x):
    hi = x.astype(BF16)
    r = x - hi.astype(F32)
    mid = r.astype(BF16)
    lo = (r - mid.astype(F32)).astype(BF16)
    return hi, mid, lo


def _dot_lx3(l_bf16, x):
    hi, mid, lo = _split3(x)
    d = lambda p: jnp.dot(l_bf16, p, preferred_element_type=F32)
    return d(hi) + d(mid) + d(lo)


def _dot_x2r(x, r_bf16):
    hi, lo = _split2(x)
    d = lambda p: jnp.dot(p, r_bf16, preferred_element_type=F32)
    return d(hi) + d(lo)


def _dot_x3r(x, r_bf16):
    hi, mid, lo = _split3(x)
    d = lambda p: jnp.dot(p, r_bf16, preferred_element_type=F32)
    return d(hi) + d(mid) + d(lo)


def _to_wide(m, lane_blk):
    s = [m[c * CHUNK:(c + 1) * CHUNK, :] for c in range(SB_CHUNKS)]
    out = s[SB_CHUNKS - 1]
    for c in range(SB_CHUNKS - 2, -1, -1):
        out = jnp.where(lane_blk == c, s[c], out)
    return out


def _to_block_diag(w, lane_blk):
    wb = w.astype(BF16)
    zero = jnp.zeros_like(wb)
    return jnp.concatenate([jnp.where(lane_blk == c, wb, zero) for c in range(SB_CHUNKS)], axis=0)


def _neumann_inverses(xs, eye_w, lane_blk):
    bd = lambda w: _to_block_diag(w, lane_blk)
    mm = lambda a, b: jnp.dot(a.astype(BF16), b, preferred_element_type=F32)
    xw = [_to_wide(x, lane_blk) for x in xs]
    p = [eye_w + w for w in xw]
    xk = [mm(w, bd(w)) for w in xw]
    yield
    for _ in range(4):
        both = [mm(a, jnp.concatenate([bd(a), bd(b)], axis=1)) for a, b in zip(xk, p)]
        xk = [t[:, 0:SBR] for t in both]
        p = [b + t[:, SBR:2 * SBR] for b, t in zip(p, both)]
        yield
    p = [b + mm(a, bd(b)) for a, b in zip(xk, p)]
    return [bd(b) for b in p]


def _drain(gen):
    try:
        while True:
            next(gen)
    except StopIteration as stop:
        return stop.value


def _interleave(gens):
    live = list(gens)
    while live:
        for g in list(live):
            try:
                next(g)
            except StopIteration:
                live.remove(g)


def _rows(i, n):
    return slice(i * n, (i + 1) * n)


def _row(i, n):
    return slice(i * n, i * n + 1)


def _silu(x):
    return x * jax.nn.sigmoid(x)


def _softplus(x):
    return jnp.maximum(x, 0.0) + jnp.log1p(jnp.exp(-jnp.abs(x)))


def _iota2(shape, dim):
    return lax.broadcasted_iota(jnp.int32, shape, dim)


def _tri_masks(n):
    r = _iota2((n, n), 0)
    c = _iota2((n, n), 1)
    same = (r // CHUNK) == (c // CHUNK)
    incl = same & (r >= c)
    strict = same & (r > c)
    tri = jnp.where(incl, 1.0, 0.0).astype(BF16)
    return incl, strict, tri


def _wide_consts():
    r = _iota2((CHUNK, SBR), 0)
    c = _iota2((CHUNK, SBR), 1)
    return jnp.where(c % CHUNK == r, 1.0, 0.0).astype(F32), c // CHUNK


def _normed_input(h_ref, npre_ref, t_idx, rows, pad_rows, r0=0, n=None):
    n = rows if n is None else n
    h = h_ref[0, r0:r0 + n, :]
    ms = jnp.mean(h * h, axis=-1, keepdims=True)
    hn = h * lax.rsqrt(ms + NORM_EPS) * npre_ref[...]
    gid = t_idx * rows + r0 + _iota2((n, 1), 0)
    valid = gid >= pad_rows
    hn = jnp.where(valid, hn, 0.0)
    return hn.astype(BF16), valid


def _chunk_last(x):
    return jnp.concatenate(
        [jnp.broadcast_to(x[(c + 1) * CHUNK - 1:(c + 1) * CHUNK, :], (CHUNK, x.shape[1]))
         for c in range(SB_CHUNKS)], axis=0)


def _pipeline(n, produce, consume):
    produce(0)
    for j in range(n):
        if j + 1 < n:
            produce(j + 1)
        consume(j)


def _causal_conv_block(ubuf, cw_ref, lo, rows, r0=0):
    full = ubuf[r0:r0 + CARRY + rows, lo:lo + LANES]
    tap = lambda j: cw_ref[j:j + 1, lo:lo + LANES]
    prev = pltpu.roll(full, 1, axis=0)
    near = full * tap(3) + prev * tap(2)
    far = full * tap(1) + prev * tap(0)
    return (near + pltpu.roll(far, 2, axis=0))[CARRY:]


def _deltanet_kernel(nsb, pad_rows, h_ref, npre_ref, w_ref, cw_ref, hp_ref, nw_ref, o_ref,
                     hn_s, ubuf, q_s, k_s, v_s, z_s, gb_s, gl_s, u_s, w_s, st_ref):
    rows = nsb * SBR
    t_idx = pl.program_id(1)

    @pl.when(t_idx == 0)
    def _():
        ubuf[0:CARRY, :] = jnp.zeros((CARRY, 3 * D_A), F32)
        st_ref[...] = jnp.zeros(st_ref.shape, F32)

    n_conv = 3 * D_A // PW
    n_z = D_A // PW
    incl, strict, tri = _tri_masks(SBR)
    eye_w, lane_blk = _wide_consts()
    scale = DH_A ** -0.5
    heads = range(H_A)
    hs = [slice(hd * DH_A, (hd + 1) * DH_A) for hd in heads]

    def front(sb):
        r0 = sb * SBR
        rs = slice(r0, r0 + SBR)
        hn, valid = _normed_input(h_ref, npre_ref, t_idx, rows, pad_rows, r0, SBR)
        hn_s[rs, :] = hn
        yield
        for j in range(n_conv + n_z + 1):
            cols = slice(j * PW, (j + 1) * PW) if j < n_conv + n_z else slice(4 * D_A, WA)
            p = jnp.dot(hn_s[rs, :], w_ref[:, cols], preferred_element_type=F32)
            if j < n_conv:
                ubuf[CARRY + r0:CARRY + r0 + SBR, cols] = p
                for blk_i in range(j * PW // LANES, (j + 1) * PW // LANES):
                    y = _silu(_causal_conv_block(ubuf, cw_ref, blk_i * LANES, SBR, r0))
                    if blk_i < 2 * H_A:
                        y = y * lax.rsqrt(jnp.sum(y * y, axis=-1, keepdims=True) + 1e-6)
                    dst = (q_s, k_s, v_s)[blk_i // H_A]
                    dst[rs, hs[blk_i % H_A]] = y
            elif j < n_conv + n_z:
                z_s[rs, (j - n_conv) * PW:(j - n_conv + 1) * PW] = p
            else:
                g = -jnp.exp(hp_ref[0:1, :]) * _softplus(p[:, 0:LANES] + hp_ref[1:2, :])
                gb_s[rs, 0:LANES] = jnp.where(valid, g, 0.0)
                gb_s[rs, LANES:2 * LANES] = jnp.where(
                    valid, jax.nn.sigmoid(p[:, LANES:2 * LANES]), 0.0)
            yield

    def block(sb):
        rs = slice(sb * SBR, (sb + 1) * SBR)
        g_sb = gb_s[rs, 0:LANES]
        gc = _dot_lx3(tri, g_sb)
        gl = _chunk_last(gc)
        gl_s[rs, :] = gl
        gct = gc.T
        beta_sb = gb_s[rs, LANES:2 * LANES]
        yield
        for first in range(0, H_A, DELTA_GROUP):
            yield from block_heads(rs, range(first, first + DELTA_GROUP), gc, gct, gl, beta_sb)

    def block_heads(rs, grp, gc, gct, gl, beta_sb):
        k = {hd: k_s[rs, hs[hd]] for hd in grp}
        gcol = {hd: gc[:, hd:hd + 1] for hd in grp}
        bcol = {hd: beta_sb[:, hd:hd + 1] for hd in grp}
        dmask = {hd: jnp.where(incl, jnp.exp(jnp.where(incl, gcol[hd] - gct[hd:hd + 1, :], 0.0)), 0.0)
                 for hd in grp}
        yield
        kb = {hd: k[hd] * bcol[hd] for hd in grp}
        a_neg = [jnp.where(strict, -(_dot_nt(kb[hd], k[hd]) * dmask[hd]), 0.0) for hd in grp]
        yield
        t_inv = dict(zip(grp, (yield from _neumann_inverses(a_neg, eye_w, lane_blk))))
        yield
        eg = {hd: jnp.exp(gcol[hd]) for hd in grp}
        sol = {hd: _dot(t_inv[hd], jnp.concatenate([v_s[rs, hs[hd]] * bcol[hd], kb[hd] * eg[hd]],
                                                   axis=1)) for hd in grp}
        yield
        qs = {hd: q_s[rs, hs[hd]] * scale for hd in grp}
        attn = {hd: _dot_nt(qs[hd], k[hd]) * dmask[hd] for hd in grp}
        yield
        asol = {hd: _dot(attn[hd], sol[hd]) for hd in grp}
        yield
        for hd in grp:
            u_s[rs, hs[hd]] = sol[hd][:, 0:DH_A]
            w_s[rs, hs[hd]] = sol[hd][:, DH_A:2 * DH_A]
            v_s[rs, hs[hd]] = asol[hd][:, 0:DH_A]
            q_s[rs, hs[hd]] = qs[hd] * eg[hd] - asol[hd][:, DH_A:2 * DH_A]
            k_s[rs, hs[hd]] = k[hd] * jnp.exp(gl[:, hd:hd + 1] - gcol[hd])

    def scan(sb):
        for c in range(sb * SB_CHUNKS, (sb + 1) * SB_CHUNKS):
            rs = _rows(c, CHUNK)
            state = [st_ref[hd] for hd in heads]
            res = [_dot(jnp.concatenate([w_s[rs, hs[hd]], q_s[rs, hs[hd]]], axis=0), state[hd])
                   for hd in heads]
            yield
            v_new = [u_s[rs, hs[hd]] - res[hd][0:CHUNK] for hd in heads]
            upd = [_dot_tn(k_s[rs, hs[hd]], v_new[hd]) for hd in heads]
            yield
            for hd in heads:
                v_s[rs, hs[hd]] = v_s[rs, hs[hd]] + res[hd][CHUNK:2 * CHUNK]
                g_tot = jnp.exp(gl_s[_row(c, CHUNK), hd:hd + 1])
                st_ref[hd] = state[hd] * g_tot + upd[hd]
            yield

    def finish(sb):
        rs = slice(sb * SBR, (sb + 1) * SBR)
        for hd in heads:
            o = v_s[rs, hs[hd]]
            o = o * lax.rsqrt(jnp.mean(o * o, axis=-1, keepdims=True) + NORM_EPS) * nw_ref[...]
            o_ref[0, rs, hs[hd]] = (o * _silu(z_s[rs, hs[hd]])).astype(o_ref.dtype)
            yield

    _drain(front(0))
    for step in range(nsb + 2):
        stages = []
        if step < nsb:
            stages.append(block(step))
        if step + 1 < nsb:
            stages.append(front(step + 1))
        if 1 <= step <= nsb:
            stages.append(scan(step - 1))
        if step >= 2:
            stages.append(finish(step - 2))
        _interleave(stages)
    ubuf[0:CARRY, :] = ubuf[rows:rows + CARRY, :]


def _seg_sum(x, seg):
    width = seg.shape[0]
    outs = []
    for b in range(x.shape[1] // width):
        outs.append(_dot_x2r(x[:, b * width:(b + 1) * width], seg))
    return jnp.concatenate(outs, axis=1)


def _rwkv_kernel(nsb, pad_rows, h_ref, npre_ref, w_ref, mu_ref, w2_ref, a2_ref, vp_ref, o_ref,
                 hn_s, pbuf, r_s, k_s, v_s, a_s, b_s, lw_s, g_s, y_s, u0_s, be_s, ke_s, st_ref):
    rows = nsb * SBR
    t_idx = pl.program_id(1)

    @pl.when(t_idx == 0)
    def _():
        pbuf[0:CARRY, :] = jnp.zeros((CARRY, WB), F32)
        st_ref[...] = jnp.zeros(st_ref.shape, F32)

    hn, _ = _normed_input(h_ref, npre_ref, t_idx, rows, pad_rows)
    hn_s[...] = hn

    def mixed(lo, width):
        full = pbuf[0:CARRY + rows, lo:lo + width]
        cur = full[CARRY:]
        prev = pltpu.roll(full, 1, axis=0)[CARRY:]
        return cur + (prev - cur) * mu_ref[:, lo:lo + width]

    w0 = vp_ref[0:1, :]
    a0 = vp_ref[1:2, :]
    k_k = vp_ref[2:3, :]
    k_a = vp_ref[3:4, :]
    r_k = vp_ref[4:5, :]
    ln_w = vp_ref[5:6, :]
    ln_b = vp_ref[6:7, :]

    hr = _iota2((LANES, LANES), 0) // DH_B
    hc = _iota2((LANES, LANES), 1) // DH_B
    bd = hr == hc
    sr = _iota2((PW, PW), 0) // DH_B
    sc = _iota2((PW, PW), 1) // DH_B
    seg = jnp.where(sr == sc, 1.0, 0.0).astype(BF16)

    col_lo = (4 * D_B, D_B, 0, 2 * D_B, 3 * D_B)
    col_w = (2 * LANES, D_B, D_B, D_B, D_B)

    def project(j):
        cols = slice(col_lo[j], col_lo[j] + col_w[j])
        pbuf[CARRY:CARRY + rows, cols] = jnp.dot(hn_s[...], w_ref[:, cols],
                                                 preferred_element_type=F32)

    def activate(j):
        if j == 0:
            w_lo = mixed(4 * D_B, LANES)
            a_lo = mixed(4 * D_B + LANES, LANES)
            lw_s[...] = -jnp.exp(-_softplus(-(w0 + _dot(jnp.tanh(w_lo), w2_ref[...]))) - 0.5)
            u0_s[...] = jax.nn.sigmoid(a0 + _dot(a_lo, a2_ref[...]))
        elif j == 1:
            k = mixed(D_B, D_B)
            a_lr = u0_s[...]
            kk = k * k_k
            kk = kk * lax.rsqrt(_seg_sum(kk * kk, seg) + 1e-6)
            k_s[...] = k * (1.0 + (a_lr - 1.0) * k_a)
            a_s[...] = -kk
            b_s[...] = kk * a_lr
        elif j == 2:
            r_s[...] = mixed(0, D_B)
        elif j == 3:
            v_s[...] = mixed(2 * D_B, D_B)
        else:
            g_s[...] = _silu(mixed(3 * D_B, D_B))

    _pipeline(len(col_lo), project, activate)
    pbuf[0:CARRY, :] = pbuf[rows:rows + CARRY, :]

    incl, strict, tri = _tri_masks(SBR)
    eye_w, lane_blk = _wide_consts()
    lane = _iota2((SBR, LANES), 1)
    half = (lane < DH_B, lane >= DH_B)

    def block(sb):
        rs = _rows(sb, SBR)
        lw = lw_s[rs, :]
        cl = _dot_lx3(tri, lw)
        cl_end = _chunk_last(cl)
        e_neg = jnp.exp(-cl)
        e_end = jnp.exp(cl_end - cl)
        a_t = a_s[rs, :] * jnp.exp(cl - lw)
        r_t = r_s[rs, :] * jnp.exp(cl)
        b_c = b_s[rs, :]
        k_c = k_s[rs, :]
        v_c = v_s[rs, :]
        b_t = b_c * e_neg
        k_t = k_c * e_neg
        be_s[rs, :] = b_c * e_end
        ke_s[rs, :] = k_c * e_end
        lw_s[rs, :] = jnp.exp(cl_end)
        yield
        for first in range(0, H_B, RWKV_GROUP):
            yield from block_heads(rs, range(first, first + RWKV_GROUP), a_t, r_t, b_t, k_t, v_c)

    def block_heads(rs, heads, a_t, r_t, b_t, k_t, v_c):
        ls = {hd: slice((hd // 2) * LANES, (hd // 2 + 1) * LANES) for hd in heads}
        am = {hd: jnp.where(half[hd % 2], a_t[:, ls[hd]], 0.0) for hd in heads}
        rm = {hd: jnp.where(half[hd % 2], r_t[:, ls[hd]], 0.0) for hd in heads}
        quad = {hd: _dot_nt(jnp.concatenate([am[hd], rm[hd]], axis=0),
                            jnp.concatenate([b_t[:, ls[hd]], k_t[:, ls[hd]]], axis=0))
                for hd in heads}
        yield
        ab = [jnp.where(strict, quad[hd][0:SBR, 0:SBR], 0.0) for hd in heads]
        t_inv = dict(zip(heads, (yield from _neumann_inverses(ab, eye_w, lane_blk))))
        yield
        akv = {hd: _dot(jnp.where(strict, quad[hd][0:SBR, SBR:2 * SBR], 0.0), v_c[:, ls[hd]])
               for hd in heads}
        yield
        sol = {hd: _dot(t_inv[hd], jnp.concatenate([am[hd], akv[hd]], axis=1)) for hd in heads}
        yield
        rbsol = {hd: _dot(jnp.where(incl, quad[hd][SBR:2 * SBR, 0:SBR], 0.0), sol[hd])
                 for hd in heads}
        rkv = {hd: _dot(jnp.where(incl, quad[hd][SBR:2 * SBR, SBR:2 * SBR], 0.0), v_c[:, ls[hd]])
               for hd in heads}
        yield
        for h0 in list(heads)[0::2]:
            h1 = h0 + 1
            a_s[rs, ls[h0]] = sol[h0][:, 0:LANES] + sol[h1][:, 0:LANES]
            b_s[rs, ls[h0]] = rm[h0] + rbsol[h0][:, 0:LANES] + rm[h1] + rbsol[h1][:, 0:LANES]
            u0_s[rs, ls[h0]] = jnp.where(half[0], sol[h0][:, LANES:2 * LANES],
                                         sol[h1][:, LANES:2 * LANES])
            y_s[rs, ls[h0]] = jnp.where(half[0], rbsol[h0][:, LANES:2 * LANES] + rkv[h0],
                                        rbsol[h1][:, LANES:2 * LANES] + rkv[h1])

    def scan(sb):
        pairs = range(H_B // 2)
        ls = [slice(pr * LANES, (pr + 1) * LANES) for pr in pairs]
        for c in range(sb * SB_CHUNKS, (sb + 1) * SB_CHUNKS):
            rs = _rows(c, CHUNK)
            state = [st_ref[pr] for pr in pairs]
            res = [_dot_nt(jnp.concatenate([a_s[rs, ls[pr]], b_s[rs, ls[pr]]], axis=0), state[pr])
                   for pr in pairs]
            yield
            u = [res[pr][0:CHUNK] + u0_s[rs, ls[pr]] for pr in pairs]
            upd = [_dot_tn(jnp.concatenate([u[pr], v_s[rs, ls[pr]]], axis=0),
                           jnp.concatenate([be_s[rs, ls[pr]], ke_s[rs, ls[pr]]], axis=0))
                   for pr in pairs]
            yield
            for pr in pairs:
                y_s[rs, ls[pr]] = y_s[rs, ls[pr]] + res[pr][CHUNK:2 * CHUNK]
                gam = lw_s[_row(c, CHUNK), ls[pr]]
                st_ref[pr] = jnp.where(bd, state[pr] * gam + upd[pr], 0.0)
            yield

    def finish(sb):
        rs = _rows(sb, SBR)
        y = y_s[rs, :]
        mean = _seg_sum(y, seg) * (1.0 / DH_B)
        yield
        yc = y - mean
        var = _seg_sum(yc * yc, seg) * (1.0 / DH_B)
        yield
        yn = yc * lax.rsqrt(var + GN_EPS_B) * ln_w + ln_b
        bonus = _seg_sum(r_s[rs, :] * k_s[rs, :] * r_k, seg) * v_s[rs, :]
        yield
        o_ref[0, rs, :] = ((yn + bonus) * g_s[rs, :]).astype(o_ref.dtype)

    for step in range(nsb + 2):
        stages = []
        if step < nsb:
            stages.append(block(step))
        if 1 <= step <= nsb:
            stages.append(scan(step - 1))
        if step >= 2:
            stages.append(finish(step - 2))
        _interleave(stages)


def _mamba_kernel(nc, pad_rows, h_ref, npre_ref, w_ref, cw_ref, cb_ref, hp_ref, vp_ref, o_ref,
                  hn_s, ubuf, z_s, x_s, bm_s, cm_s, dt_s, a_s, e_in_s, st_ref):
    rows = nc * CHUNK
    t_idx = pl.program_id(1)

    @pl.when(t_idx == 0)
    def _():
        ubuf[0:CARRY, :] = jnp.zeros((CARRY, WC_CONV), F32)
        st_ref[...] = jnp.zeros(st_ref.shape, F32)

    hn, valid = _normed_input(h_ref, npre_ref, t_idx, rows, pad_rows)
    hn_s[...] = hn
    n_conv = WC_CONV // PW
    n_z = D_CP // PW

    def project(j):
        if j < n_conv:
            cols = slice(D_CP + j * PW, D_CP + (j + 1) * PW)
            ubuf[CARRY:CARRY + rows, j * PW:(j + 1) * PW] = jnp.dot(
                hn_s[...], w_ref[:, cols], preferred_element_type=F32)
        elif j < n_conv + n_z:
            cols = slice((j - n_conv) * PW, (j - n_conv + 1) * PW)
            z_s[:, cols] = jnp.dot(hn_s[...], w_ref[:, cols], preferred_element_type=F32)
        else:
            dtp = jnp.dot(hn_s[...], w_ref[:, D_CP + WC_CONV:WC], preferred_element_type=F32)
            dt = jnp.where(valid, _softplus(dtp + hp_ref[0:1, :]), 0.0)
            dt_s[...] = dt
            a_s[...] = -jnp.exp(hp_ref[1:2, :]) * dt

    def activate(j):
        if j >= n_conv:
            return
        for blk_i in range(j * PW // LANES, (j + 1) * PW // LANES):
            lo = blk_i * LANES
            y = _silu(_causal_conv_block(ubuf, cw_ref, lo, rows) + cb_ref[:, lo:lo + LANES])
            if lo < D_CP:
                x_s[:, lo:lo + LANES] = y
            elif lo < D_CP + G_C * N_C:
                bm_s[:, lo - D_CP:lo - D_CP + LANES] = jnp.where(valid, y, 0.0)
            else:
                o2 = lo - D_CP - G_C * N_C
                cm_s[:, o2:o2 + LANES] = jnp.where(valid, y, 0.0)

    _pipeline(n_conv + n_z + 1, project, activate)
    ubuf[0:CARRY, :] = ubuf[rows:rows + CARRY, :]

    incl, _, tri = _tri_masks(SBR)
    glane = _iota2((SBR, GW), 1)
    lane = _iota2((SBR, LANES), 1)
    prow = _iota2((GW, N_C), 0)
    d_skip = vp_ref[0:1, :]

    def expand_heads(f):
        outs = []
        for b in range(D_CP // LANES):
            gi, hb = divmod(b, GW // LANES)
            h0 = gi * HPG + 2 * hb
            second = f[:, h0 + 1:h0 + 2] if hb == 0 else 0.0
            outs.append(jnp.where(lane < DH_C, f[:, h0:h0 + 1], second))
        return jnp.concatenate(outs, axis=1)

    groups = range(G_C)
    gs = [slice(gi * GW, (gi + 1) * GW) for gi in groups]
    ns = [slice(gi * N_C, (gi + 1) * N_C) for gi in groups]

    def block(sb):
        rs = _rows(sb, SBR)
        a_sb = a_s[rs, :]
        acs = _dot_lx3(tri, a_sb)
        a_end = _chunk_last(acs)
        dt_sb = dt_s[rs, :]
        acst = acs.T
        dtt = dt_sb.T
        yield
        e_in_s[rs, :] = expand_heads(jnp.exp(acs))
        a_s[rs, :] = jnp.exp(a_end)
        xs = x_s[rs, :]
        x_s[rs, :] = xs * expand_heads(dt_sb * jnp.exp(a_end - acs))
        yield
        cb = [_dot_nt(cm_s[rs, ns[gi]], bm_s[rs, ns[gi]]) for gi in groups]
        yield
        m = []
        for hd in range(H_C):
            decay = jnp.exp(jnp.where(incl, acs[:, hd:hd + 1] - acst[hd:hd + 1, :], 0.0))
            m.append(jnp.where(incl, cb[hd // HPG] * decay, 0.0) * dtt[hd:hd + 1, :])
            if hd % HPG == HPG - 1:
                yield
        in_head = [(glane >= hh * DH_C) & (glane < (hh + 1) * DH_C) for hh in range(HPG)]
        terms = [_dot(m[hd], jnp.where(in_head[hd % HPG], xs[:, gs[hd // HPG]], 0.0))
                 for hd in range(H_C)]
        yield
        for gi in groups:
            y = d_skip[:, gs[gi]] * xs[:, gs[gi]]
            for hh in range(HPG):
                y = y + terms[gi * HPG + hh]
            o_ref[0, rs, gs[gi]] = y

    def scan(sb):
        for c in range(sb * SB_CHUNKS, (sb + 1) * SB_CHUNKS):
            rs = _rows(c, CHUNK)
            state = [st_ref[gi] for gi in groups]
            y_off = [_dot_nt(cm_s[rs, ns[gi]], state[gi]) for gi in groups]
            upd = [_dot_tn(x_s[rs, gs[gi]], bm_s[rs, ns[gi]]) for gi in groups]
            yield
            for gi in groups:
                o_ref[0, rs, gs[gi]] = o_ref[0, rs, gs[gi]] + y_off[gi] * e_in_s[rs, gs[gi]]
                scale_rows = jnp.zeros((GW, N_C), F32)
                for hh in range(HPG):
                    hd = gi * HPG + hh
                    scale_rows = jnp.where((prow >= hh * DH_C) & (prow < (hh + 1) * DH_C),
                                           a_s[_row(c, CHUNK), hd:hd + 1], scale_rows)
                st_ref[gi] = state[gi] * scale_rows + upd[gi]
            yield

    def finish(sb):
        rs = _rows(sb, SBR)
        for gi in groups:
            y = o_ref[0, rs, gs[gi]] * _silu(z_s[rs, gs[gi]])
            ms = jnp.sum(y * y, axis=-1, keepdims=True) * (1.0 / (HPG * DH_C))
            o_ref[0, rs, gs[gi]] = y * lax.rsqrt(ms + NORM_EPS) * vp_ref[1:2, gs[gi]]
            yield

    for step in range(nc // SB_CHUNKS + 2):
        stages = []
        if step < nc // SB_CHUNKS:
            stages.append(block(step))
        if 1 <= step <= nc // SB_CHUNKS:
            stages.append(scan(step - 1))
        if step >= 2:
            stages.append(finish(step - 2))
        _interleave(stages)


def _out_kernel(h_ref, oa_ref, ob_ref, oc_ref, wa_ref, wb_ref, wc_ref, npost_ref, o_ref):
    out = (_dot(oa_ref[0], wa_ref[...]) + _dot(ob_ref[0], wb_ref[...])
           + _dot(oc_ref[0], wc_ref[...]))
    ms = jnp.mean(out * out, axis=-1, keepdims=True)
    o_ref[0] = h_ref[0] + out * lax.rsqrt(ms + NORM_EPS) * npost_ref[...]


def _pad_lanes(x, width=LANES):
    return jnp.pad(x, [(0, 0)] * (x.ndim - 1) + [(0, width - x.shape[-1])])


def _group_pad(x):
    lead = x.shape[:-1]
    xg = x.reshape(lead + (G_C, HPG * DH_C))
    xg = jnp.pad(xg, [(0, 0)] * len(lead) + [(0, 0), (0, GW - HPG * DH_C)])
    return xg.reshape(lead + (D_CP,))


def _tiling(seq):
    n_sb = -(-(N_META + seq) // SBR)
    best = None
    for nsb in range(MAX_TILE_SB, 0, -1):
        total = n_sb + (-n_sb) % nsb
        if best is None or total < best[1]:
            best = (nsb, total)
    nsb, total = best
    lp = total * SBR
    return nsb, lp - N_META - seq, lp


def _const_spec(shape):
    return pl.BlockSpec(shape, lambda b, t: (0,) * len(shape))


def _mixer_call(body, rows, bsz, lp, out_w, out_dtype, consts, scratch, name, h):
    tile = pl.BlockSpec((1, rows, D_MODEL), lambda b, t: (b, t, 0))
    return pl.pallas_call(
        body,
        out_shape=jax.ShapeDtypeStruct((bsz, lp, out_w), out_dtype),
        grid=(bsz, lp // rows),
        in_specs=[tile] + [_const_spec(c.shape) for c in consts],
        out_specs=pl.BlockSpec((1, rows, out_w), lambda b, t: (b, t, 0)),
        scratch_shapes=scratch,
        compiler_params=pltpu.CompilerParams(
            dimension_semantics=("arbitrary", "arbitrary"), vmem_limit_bytes=VMEM_LIMIT),
        name=name,
    )(h, *consts)


def kernel(x, meta_tokens, norm_pre, norm_post, w_in, w_out, dn_conv, dn_A_log, dn_dt_bias, dn_norm, rw_mu, rw_w0, rw_w2, rw_a0, rw_a2, rw_k_k, rw_k_a, rw_r_k, rw_ln_w, rw_ln_b, mb_conv, mb_conv_b, mb_dt_bias, mb_A_log, mb_D, mb_norm):
    bsz, seq, _ = x.shape
    depth = w_in.shape[0]
    nsb, pad_rows, lp = _tiling(seq)
    rows = nsb * SBR
    nc = nsb * SB_CHUNKS
    meta = jnp.broadcast_to(meta_tokens.astype(x.dtype)[None], (bsz, N_META, D_MODEL))
    h = jnp.concatenate([jnp.zeros((bsz, pad_rows, D_MODEL), x.dtype), meta, x], axis=1)

    vm = lambda r, c: pltpu.VMEM((r, c), F32)
    hn_scratch = pltpu.VMEM((rows, D_MODEL), BF16)
    for l in range(depth):
        npre = norm_pre[l][None, :]
        wl = w_in[l]
        w_a = jnp.concatenate([wl[:, :4 * D_A], _pad_lanes(wl[:, 4 * D_A:4 * D_A + H_A]),
                               _pad_lanes(wl[:, 4 * D_A + H_A:P_A])], axis=1).astype(BF16)
        hp_a = jnp.stack([_pad_lanes(dn_A_log[l]), _pad_lanes(dn_dt_bias[l])])
        hp_a = jnp.pad(hp_a, ((0, 6), (0, 0)))
        o_a = _mixer_call(
            functools.partial(_deltanet_kernel, nsb, pad_rows), rows, bsz, lp, D_A, BF16,
            [npre, w_a, dn_conv[l], hp_a, dn_norm[l][None, :]],
            [hn_scratch, vm(rows + CARRY, 3 * D_A), vm(rows, D_A), vm(rows, D_A), vm(rows, D_A),
             vm(rows, D_A), vm(rows, 2 * LANES), vm(rows, LANES), vm(rows, D_A), vm(rows, D_A),
             pltpu.VMEM((H_A, DH_A, DH_A), F32)],
            "deltanet", h)
        wb_l = wl[:, P_A:P_A + P_B]
        w_b = jnp.concatenate([wb_l[:, :4 * D_B], _pad_lanes(wb_l[:, 4 * D_B:4 * D_B + R_W]),
                               _pad_lanes(wb_l[:, 4 * D_B + R_W:])], axis=1).astype(BF16)
        mu = rw_mu[l]
        mu_b = jnp.concatenate([mu[:4 * D_B], _pad_lanes(mu[4 * D_B:4 * D_B + R_W]),
                                _pad_lanes(mu[4 * D_B + R_W:])])[None, :]
        w2p = jnp.pad(rw_w2[l], ((0, LANES - R_W), (0, 0))).astype(BF16)
        a2p = jnp.pad(rw_a2[l], ((0, LANES - R_A), (0, 0))).astype(BF16)
        vp_b = jnp.stack([rw_w0[l], rw_a0[l], rw_k_k[l], rw_k_a[l], rw_r_k[l].reshape(D_B),
                          rw_ln_w[l], rw_ln_b[l], jnp.zeros((D_B,), F32)])
        o_b = _mixer_call(
            functools.partial(_rwkv_kernel, nsb, pad_rows), rows, bsz, lp, D_B, BF16,
            [npre, w_b, mu_b, w2p, a2p, vp_b],
            [hn_scratch, vm(rows + CARRY, WB)] + [vm(rows, D_B)] * 11
            + [pltpu.VMEM((H_B // 2, LANES, LANES), F32)],
            "rwkv7", h)
        wc_l = wl[:, P_A + P_B:]
        w_c = jnp.concatenate([
            _group_pad(wc_l[:, :D_C]), _group_pad(wc_l[:, D_C:2 * D_C]),
            wc_l[:, 2 * D_C:2 * D_C + 2 * G_C * N_C], _pad_lanes(wc_l[:, 2 * D_C + 2 * G_C * N_C:]),
        ], axis=1).astype(BF16)
        cw_c = jnp.concatenate([_group_pad(mb_conv[l][:, :D_C]), mb_conv[l][:, D_C:]], axis=1)
        cb_c = jnp.concatenate([_group_pad(mb_conv_b[l][:D_C]), mb_conv_b[l][D_C:]])[None, :]
        hp_c = jnp.pad(jnp.stack([_pad_lanes(mb_dt_bias[l]), _pad_lanes(mb_A_log[l])]),
                       ((0, 6), (0, 0)))
        vp_c = jnp.pad(jnp.stack([_group_pad(jnp.repeat(mb_D[l], DH_C)), _group_pad(mb_norm[l])]),
                       ((0, 6), (0, 0)))
        o_c = _mixer_call(
            functools.partial(_mamba_kernel, nc, pad_rows), rows, bsz, lp, D_CP, F32,
            [npre, w_c, cw_c, cb_c, hp_c, vp_c],
            [hn_scratch, vm(rows + CARRY, WC_CONV), vm(rows, D_CP), vm(rows, D_CP),
             vm(rows, G_C * N_C),
             vm(rows, G_C * N_C), vm(rows, LANES), vm(rows, LANES), vm(rows, D_CP),
             pltpu.VMEM((G_C, GW, N_C), F32)],
            "mamba2", h)
        wo = w_out[l]
        wo_a = wo[:D_A].astype(BF16)
        wo_b = wo[D_A:D_A + D_B].astype(BF16)
        wo_c = jnp.pad(wo[D_A + D_B:].reshape(G_C, HPG * DH_C, D_MODEL),
                       ((0, 0), (0, GW - HPG * DH_C), (0, 0))).reshape(D_CP, D_MODEL).astype(BF16)
        row_spec = lambda w: pl.BlockSpec((1, rows, w), lambda b, t: (b, t, 0))
        h = pl.pallas_call(
            _out_kernel,
            out_shape=jax.ShapeDtypeStruct((bsz, lp, D_MODEL), F32),
            grid=(bsz, lp // rows),
            in_specs=[row_spec(D_MODEL), row_spec(D_A), row_spec(D_B), row_spec(D_CP),
                      _const_spec(wo_a.shape), _const_spec(wo_b.shape), _const_spec(wo_c.shape),
                      _const_spec((1, D_MODEL))],
            out_specs=row_spec(D_MODEL),
            compiler_params=pltpu.CompilerParams(
                dimension_semantics=("arbitrary", "arbitrary"), vmem_limit_bytes=VMEM_LIMIT),
            name="out_proj",
        )(h, o_a, o_b, o_c, wo_a, wo_b, wo_c, norm_post[l][None, :])
    return h[:, pad_rows + N_META:]
```

```python
import functools

import jax
import jax.numpy as jnp
from jax import lax
from jax.experimental import pallas as pl
from jax.experimental.pallas import tpu as pltpu

F32 = jnp.float32
BF16 = jnp.bfloat16

D_MODEL = 1024
CHUNK = 64
SB_CHUNKS = 4
SBR = SB_CHUNKS * CHUNK
N_META = 16
CONV_K = 4
NORM_EPS = 1e-6
LANES = 128
CARRY = 8

H_A, DH_A = 6, 128
D_A = H_A * DH_A
H_B, DH_B = 8, 64
D_B = H_B * DH_B
R_W = R_A = 64
GN_EPS_B = 64e-5
H_C, DH_C = 12, 64
D_C = H_C * DH_C
G_C, N_C = 4, 128
HPG = H_C // G_C
GW = 256
D_CP = G_C * GW
P_A = 4 * D_A + 2 * H_A
P_B = 4 * D_B + R_W + R_A
P_C = 2 * D_C + 2 * G_C * N_C + H_C

WA = 4 * D_A + 2 * LANES
WB = 4 * D_B + 2 * LANES
WC_CONV = D_CP + 2 * G_C * N_C
WC = D_CP + WC_CONV + LANES

VMEM_LIMIT = 56 * 1024 * 1024
MAX_TILE_SB = 3
PW = 256
DELTA_GROUP = 3
RWKV_GROUP = 8
MAMBA_GROUP = 2


def _dot(a, b):
    return jnp.dot(a.astype(BF16), b.astype(BF16), preferred_element_type=F32)


def _dot_nt(a, b):
    return lax.dot_general(a.astype(BF16), b.astype(BF16), (((1,), (1,)), ((), ())),
                           preferred_element_type=F32)


def _dot_tn(a, b):
    return lax.dot_general(a.astype(BF16), b.astype(BF16), (((0,), (0,)), ((), ())),
                           preferred_element_type=F32)


def _split2(x):
    hi = x.astype(BF16)
    lo = (x - hi.astype(F32)).astype(BF16)
    return hi, lo


def _split3(x):
    hi = x.astype(BF16)
    r = x - hi.astype(F32)
    mid = r.astype(BF16)
    lo = (r - mid.astype(F32)).astype(BF16)
    return hi, mid, lo


def _dot_lx3(l_bf16, x):
    hi, mid, lo = _split3(x)
    d = lambda p: jnp.dot(l_bf16, p, preferred_element_type=F32)
    return d(hi) + d(mid) + d(lo)


def _dot_x2r(x, r_bf16):
    hi, lo = _split2(x)
    d = lambda p: jnp.dot(p, r_bf16, preferred_element_type=F32)
    return d(hi) + d(lo)


def _dot_x3r(x, r_bf16):
    hi, mid, lo = _split3(x)
    d = lambda p: jnp.dot(p, r_bf16, preferred_element_type=F32)
    return d(hi) + d(mid) + d(lo)


def _to_wide(m, lane_blk):
    s = [m[c * CHUNK:(c + 1) * CHUNK, :] for c in range(SB_CHUNKS)]
    out = s[SB_CHUNKS - 1]
    for c in range(SB_CHUNKS - 2, -1, -1):
        out = jnp.where(lane_blk == c, s[c], out)
    return out


def _to_block_diag(w, lane_blk):
    wb = w.astype(BF16)
    zero = jnp.zeros_like(wb)
    return jnp.concatenate([jnp.where(lane_blk == c, wb, zero) for c in range(SB_CHUNKS)], axis=0)


def _neumann_inverses(xs, eye_w, lane_blk):
    bd = lambda w: _to_block_diag(w, lane_blk)
    mm = lambda a, b: jnp.dot(a.astype(BF16), b, preferred_element_type=F32)
    xw = [_to_wide(x, lane_blk) for x in xs]
    p = [eye_w + w for w in xw]
    xk = [mm(w, bd(w)) for w in xw]
    yield
    for _ in range(4):
        both = [mm(a, jnp.concatenate([bd(a), bd(b)], axis=1)) for a, b in zip(xk, p)]
        xk = [t[:, 0:SBR] for t in both]
        p = [b + t[:, SBR:2 * SBR] for b, t in zip(p, both)]
        yield
    p = [b + mm(a, bd(b)) for a, b in zip(xk, p)]
    return [bd(b) for b in p]


def _drain(gen):
    try:
        while True:
            next(gen)
    except StopIteration as stop:
        return stop.value


def _interleave(gens):
    live = list(gens)
    while live:
        for g in list(live):
            try:
                next(g)
            except StopIteration:
                live.remove(g)


def _rows(i, n):
    return slice(i * n, (i + 1) * n)


def _row(i, n):
    return slice(i * n, i * n + 1)


def _silu(x):
    return x * jax.nn.sigmoid(x)


def _softplus(x):
    return jnp.maximum(x, 0.0) + jnp.log1p(jnp.exp(-jnp.abs(x)))


def _iota2(shape, dim):
    return lax.broadcasted_iota(jnp.int32, shape, dim)


def _tri_masks(n):
    r = _iota2((n, n), 0)
    c = _iota2((n, n), 1)
    same = (r // CHUNK) == (c // CHUNK)
    incl = same & (r >= c)
    strict = same & (r > c)
    tri = jnp.where(incl, 1.0, 0.0).astype(BF16)
    return incl, strict, tri


def _wide_consts():
    r = _iota2((CHUNK, SBR), 0)
    c = _iota2((CHUNK, SBR), 1)
    return jnp.where(c % CHUNK == r, 1.0, 0.0).astype(F32), c // CHUNK


def _normed_input(h_ref, npre_ref, t_idx, rows, pad_rows, r0=0, n=None):
    n = rows if n is None else n
    h = h_ref[0, r0:r0 + n, :]
    ms = jnp.mean(h * h, axis=-1, keepdims=True)
    hn = h * lax.rsqrt(ms + NORM_EPS) * npre_ref[...]
    gid = t_idx * rows + r0 + _iota2((n, 1), 0)
    valid = gid >= pad_rows
    hn = jnp.where(valid, hn, 0.0)
    return hn.astype(BF16), valid


def _chunk_last(x):
    return jnp.concatenate(
        [jnp.broadcast_to(x[(c + 1) * CHUNK - 1:(c + 1) * CHUNK, :], (CHUNK, x.shape[1]))
         for c in range(SB_CHUNKS)], axis=0)


def _causal_conv_block(ubuf, cw_ref, lo, rows, r0=0):
    full = ubuf[r0:r0 + CARRY + rows, lo:lo + LANES]
    tap = lambda j: cw_ref[j:j + 1, lo:lo + LANES]
    prev = pltpu.roll(full, 1, axis=0)
    near = full * tap(3) + prev * tap(2)
    far = full * tap(1) + prev * tap(0)
    return (near + pltpu.roll(far, 2, axis=0))[CARRY:]


def _deltanet_kernel(nsb, pad_rows, h_ref, npre_ref, w_ref, cw_ref, hp_ref, nw_ref, o_ref,
                     hn_s, ubuf, q_s, k_s, v_s, z_s, gb_s, gl_s, u_s, w_s, st_ref):
    rows = nsb * SBR
    t_idx = pl.program_id(1)

    @pl.when(t_idx == 0)
    def _():
        ubuf[0:CARRY, :] = jnp.zeros((CARRY, 3 * D_A), F32)
        st_ref[...] = jnp.zeros(st_ref.shape, F32)

    n_conv = 3 * D_A // PW
    n_z = D_A // PW
    incl, strict, tri = _tri_masks(SBR)
    eye_w, lane_blk = _wide_consts()
    scale = DH_A ** -0.5
    heads = range(H_A)
    hs = [slice(hd * DH_A, (hd + 1) * DH_A) for hd in heads]

    def front(sb):
        r0 = sb * SBR
        rs = slice(r0, r0 + SBR)
        hn, valid = _normed_input(h_ref, npre_ref, t_idx, rows, pad_rows, r0, SBR)
        hn_s[rs, :] = hn
        yield
        for j in range(n_conv + n_z + 1):
            cols = slice(j * PW, (j + 1) * PW) if j < n_conv + n_z else slice(4 * D_A, WA)
            p = jnp.dot(hn_s[rs, :], w_ref[:, cols], preferred_element_type=F32)
            if j < n_conv:
                ubuf[CARRY + r0:CARRY + r0 + SBR, cols] = p
                for blk_i in range(j * PW // LANES, (j + 1) * PW // LANES):
                    y = _silu(_causal_conv_block(ubuf, cw_ref, blk_i * LANES, SBR, r0))
                    if blk_i < 2 * H_A:
                        y = y * lax.rsqrt(jnp.sum(y * y, axis=-1, keepdims=True) + 1e-6)
                    dst = (q_s, k_s, v_s)[blk_i // H_A]
                    dst[rs, hs[blk_i % H_A]] = y
            elif j < n_conv + n_z:
                z_s[rs, (j - n_conv) * PW:(j - n_conv + 1) * PW] = p
            else:
                g = -jnp.exp(hp_ref[0:1, :]) * _softplus(p[:, 0:LANES] + hp_ref[1:2, :])
                gb_s[rs, 0:LANES] = jnp.where(valid, g, 0.0)
                gb_s[rs, LANES:2 * LANES] = jnp.where(
                    valid, jax.nn.sigmoid(p[:, LANES:2 * LANES]), 0.0)
            yield

    def block(sb):
        rs = slice(sb * SBR, (sb + 1) * SBR)
        g_sb = gb_s[rs, 0:LANES]
        gc = _dot_lx3(tri, g_sb)
        gl = _chunk_last(gc)
        gl_s[rs, :] = gl
        gct = gc.T
        beta_sb = gb_s[rs, LANES:2 * LANES]
        yield
        for first in range(0, H_A, DELTA_GROUP):
            yield from block_heads(rs, range(first, first + DELTA_GROUP), gc, gct, gl, beta_sb)

    def block_heads(rs, grp, gc, gct, gl, beta_sb):
        k = {hd: k_s[rs, hs[hd]] for hd in grp}
        gcol = {hd: gc[:, hd:hd + 1] for hd in grp}
        bcol = {hd: beta_sb[:, hd:hd + 1] for hd in grp}
        dmask = {hd: jnp.where(incl, jnp.exp(jnp.where(incl, gcol[hd] - gct[hd:hd + 1, :], 0.0)), 0.0)
                 for hd in grp}
        yield
        kb = {hd: k[hd] * bcol[hd] for hd in grp}
        a_neg = [jnp.where(strict, -(_dot_nt(kb[hd], k[hd]) * dmask[hd]), 0.0) for hd in grp]
        yield
        t_inv = dict(zip(grp, (yield from _neumann_inverses(a_neg, eye_w, lane_blk))))
        yield
        eg = {hd: jnp.exp(gcol[hd]) for hd in grp}
        sol = {hd: _dot(t_inv[hd], jnp.concatenate([v_s[rs, hs[hd]] * bcol[hd], kb[hd] * eg[hd]],
                                                   axis=1)) for hd in grp}
        yield
        qs = {hd: q_s[rs, hs[hd]] * scale for hd in grp}
        attn = {hd: _dot_nt(qs[hd], k[hd]) * dmask[hd] for hd in grp}
        yield
        asol = {hd: _dot(attn[hd], sol[hd]) for hd in grp}
        yield
        for hd in grp:
            u_s[rs, hs[hd]] = sol[hd][:, 0:DH_A]
            w_s[rs, hs[hd]] = sol[hd][:, DH_A:2 * DH_A]
            v_s[rs, hs[hd]] = asol[hd][:, 0:DH_A]
            q_s[rs, hs[hd]] = qs[hd] * eg[hd] - asol[hd][:, DH_A:2 * DH_A]
            k_s[rs, hs[hd]] = k[hd] * jnp.exp(gl[:, hd:hd + 1] - gcol[hd])

    def scan(sb):
        for c in range(sb * SB_CHUNKS, (sb + 1) * SB_CHUNKS):
            rs = _rows(c, CHUNK)
            state = [st_ref[hd] for hd in heads]
            res = [_dot(jnp.concatenate([w_s[rs, hs[hd]], q_s[rs, hs[hd]]], axis=0), state[hd])
                   for hd in heads]
            yield
            v_new = [u_s[rs, hs[hd]] - res[hd][0:CHUNK] for hd in heads]
            upd = [_dot_tn(k_s[rs, hs[hd]], v_new[hd]) for hd in heads]
            yield
            for hd in heads:
                v_s[rs, hs[hd]] = v_s[rs, hs[hd]] + res[hd][CHUNK:2 * CHUNK]
                g_tot = jnp.exp(gl_s[_row(c, CHUNK), hd:hd + 1])
                st_ref[hd] = state[hd] * g_tot + upd[hd]
            yield

    def finish(sb):
        rs = slice(sb * SBR, (sb + 1) * SBR)
        for hd in heads:
            o = v_s[rs, hs[hd]]
            o = o * lax.rsqrt(jnp.mean(o * o, axis=-1, keepdims=True) + NORM_EPS) * nw_ref[...]
            o_ref[0, rs, hs[hd]] = (o * _silu(z_s[rs, hs[hd]])).astype(o_ref.dtype)
            yield

    _drain(front(0))
    for step in range(nsb + 2):
        stages = []
        if step < nsb:
            stages.append(block(step))
        if step + 1 < nsb:
            stages.append(front(step + 1))
        if 1 <= step <= nsb:
            stages.append(scan(step - 1))
        if step >= 2:
            stages.append(finish(step - 2))
        _interleave(stages)
    ubuf[0:CARRY, :] = ubuf[rows:rows + CARRY, :]


def _seg_sum(x, seg):
    width = seg.shape[0]
    outs = []
    for b in range(x.shape[1] // width):
        outs.append(_dot_x2r(x[:, b * width:(b + 1) * width], seg))
    return jnp.concatenate(outs, axis=1)


def _rwkv_kernel(nsb, pad_rows, h_ref, npre_ref, w_ref, mu_ref, w2_ref, a2_ref, vp_ref, o_ref,
                 hn_s, pbuf, r_s, k_s, v_s, a_s, b_s, lw_s, g_s, y_s, u0_s, be_s, ke_s, st_ref):
    rows = nsb * SBR
    t_idx = pl.program_id(1)

    @pl.when(t_idx == 0)
    def _():
        pbuf[0:CARRY, :] = jnp.zeros((CARRY, WB), F32)
        st_ref[...] = jnp.zeros(st_ref.shape, F32)

    def mixed(r0, n, lo, width):
        full = pbuf[r0:r0 + CARRY + n, lo:lo + width]
        cur = full[CARRY:]
        prev = pltpu.roll(full, 1, axis=0)[CARRY:]
        return cur + (prev - cur) * mu_ref[:, lo:lo + width]

    w0 = vp_ref[0:1, :]
    a0 = vp_ref[1:2, :]
    k_k = vp_ref[2:3, :]
    k_a = vp_ref[3:4, :]
    r_k = vp_ref[4:5, :]
    ln_w = vp_ref[5:6, :]
    ln_b = vp_ref[6:7, :]

    hr = _iota2((LANES, LANES), 0) // DH_B
    hc = _iota2((LANES, LANES), 1) // DH_B
    bd = hr == hc
    sr = _iota2((PW, PW), 0) // DH_B
    sc = _iota2((PW, PW), 1) // DH_B
    seg = jnp.where(sr == sc, 1.0, 0.0).astype(BF16)

    col_lo = (4 * D_B, D_B, 0, 2 * D_B, 3 * D_B)
    col_w = (2 * LANES, D_B, D_B, D_B, D_B)

    hn, _ = _normed_input(h_ref, npre_ref, t_idx, rows, pad_rows)
    hn_s[...] = hn

    def project(j):
        cols = slice(col_lo[j], col_lo[j] + col_w[j])
        pbuf[CARRY:CARRY + rows, cols] = jnp.dot(hn_s[...], w_ref[:, cols],
                                                 preferred_element_type=F32)

    def activate(j):
        if j == 0:
            w_lo = mixed(0, rows, 4 * D_B, LANES)
            a_lo = mixed(0, rows, 4 * D_B + LANES, LANES)
            lw_s[...] = -jnp.exp(-_softplus(-(w0 + _dot(jnp.tanh(w_lo), w2_ref[...]))) - 0.5)
            u0_s[...] = jax.nn.sigmoid(a0 + _dot(a_lo, a2_ref[...]))
        elif j == 1:
            k = mixed(0, rows, D_B, D_B)
            a_lr = u0_s[...]
            kk = k * k_k
            kk = kk * lax.rsqrt(_seg_sum(kk * kk, seg) + 1e-6)
            k_s[...] = k * (1.0 + (a_lr - 1.0) * k_a)
            a_s[...] = -kk
            b_s[...] = kk * a_lr
        elif j == 2:
            r_s[...] = mixed(0, rows, 0, D_B)
        elif j == 3:
            v_s[...] = mixed(0, rows, 2 * D_B, D_B)
        else:
            g_s[...] = _silu(mixed(0, rows, 3 * D_B, D_B))

    project(0)
    for j in range(len(col_lo)):
        if j + 1 < len(col_lo):
            project(j + 1)
        activate(j)

    incl, strict, tri = _tri_masks(SBR)
    eye_w, lane_blk = _wide_consts()
    lane = _iota2((SBR, LANES), 1)
    half = (lane < DH_B, lane >= DH_B)

    def block(sb):
        rs = _rows(sb, SBR)
        lw = lw_s[rs, :]
        cl = _dot_lx3(tri, lw)
        cl_end = _chunk_last(cl)
        e_neg = jnp.exp(-cl)
        e_end = jnp.exp(cl_end - cl)
        a_t = a_s[rs, :] * jnp.exp(cl - lw)
        r_t = r_s[rs, :] * jnp.exp(cl)
        b_c = b_s[rs, :]
        k_c = k_s[rs, :]
        v_c = v_s[rs, :]
        b_t = b_c * e_neg
        k_t = k_c * e_neg
        be_s[rs, :] = b_c * e_end
        ke_s[rs, :] = k_c * e_end
        lw_s[rs, :] = jnp.exp(cl_end)
        yield
        for first in range(0, H_B, RWKV_GROUP):
            yield from block_heads(rs, range(first, first + RWKV_GROUP), a_t, r_t, b_t, k_t, v_c)

    def block_heads(rs, heads, a_t, r_t, b_t, k_t, v_c):
        ls = {hd: slice((hd // 2) * LANES, (hd // 2 + 1) * LANES) for hd in heads}
        am = {hd: jnp.where(half[hd % 2], a_t[:, ls[hd]], 0.0) for hd in heads}
        rm = {hd: jnp.where(half[hd % 2], r_t[:, ls[hd]], 0.0) for hd in heads}
        quad = {hd: _dot_nt(jnp.concatenate([am[hd], rm[hd]], axis=0),
                            jnp.concatenate([b_t[:, ls[hd]], k_t[:, ls[hd]]], axis=0))
                for hd in heads}
        yield
        ab = [jnp.where(strict, quad[hd][0:SBR, 0:SBR], 0.0) for hd in heads]
        t_inv = dict(zip(heads, (yield from _neumann_inverses(ab, eye_w, lane_blk))))
        yield
        akv = {hd: _dot(jnp.where(strict, quad[hd][0:SBR, SBR:2 * SBR], 0.0), v_c[:, ls[hd]])
               for hd in heads}
        yield
        sol = {hd: _dot(t_inv[hd], jnp.concatenate([am[hd], akv[hd]], axis=1)) for hd in heads}
        yield
        rbsol = {hd: _dot(jnp.where(incl, quad[hd][SBR:2 * SBR, 0:SBR], 0.0), sol[hd])
                 for hd in heads}
        rkv = {hd: _dot(jnp.where(incl, quad[hd][SBR:2 * SBR, SBR:2 * SBR], 0.0), v_c[:, ls[hd]])
               for hd in heads}
        yield
        for h0 in list(heads)[0::2]:
            h1 = h0 + 1
            a_s[rs, ls[h0]] = sol[h0][:, 0:LANES] + sol[h1][:, 0:LANES]
            b_s[rs, ls[h0]] = rm[h0] + rbsol[h0][:, 0:LANES] + rm[h1] + rbsol[h1][:, 0:LANES]
            u0_s[rs, ls[h0]] = jnp.where(half[0], sol[h0][:, LANES:2 * LANES],
                                         sol[h1][:, LANES:2 * LANES])
            y_s[rs, ls[h0]] = jnp.where(half[0], rbsol[h0][:, LANES:2 * LANES] + rkv[h0],
                                        rbsol[h1][:, LANES:2 * LANES] + rkv[h1])

    def scan(sb):
        pairs = range(H_B // 2)
        ls = [slice(pr * LANES, (pr + 1) * LANES) for pr in pairs]
        for c in range(sb * SB_CHUNKS, (sb + 1) * SB_CHUNKS):
            rs = _rows(c, CHUNK)
            state = [st_ref[pr] for pr in pairs]
            res = [_dot_nt(jnp.concatenate([a_s[rs, ls[pr]], b_s[rs, ls[pr]]], axis=0), state[pr])
                   for pr in pairs]
            yield
            u = [res[pr][0:CHUNK] + u0_s[rs, ls[pr]] for pr in pairs]
            upd = [_dot_tn(jnp.concatenate([u[pr], v_s[rs, ls[pr]]], axis=0),
                           jnp.concatenate([be_s[rs, ls[pr]], ke_s[rs, ls[pr]]], axis=0))
                   for pr in pairs]
            yield
            for pr in pairs:
                y_s[rs, ls[pr]] = y_s[rs, ls[pr]] + res[pr][CHUNK:2 * CHUNK]
                gam = lw_s[_row(c, CHUNK), ls[pr]]
                st_ref[pr] = jnp.where(bd, state[pr] * gam + upd[pr], 0.0)
            yield

    def finish(sb):
        rs = _rows(sb, SBR)
        y = y_s[rs, :]
        mean = _seg_sum(y, seg) * (1.0 / DH_B)
        yield
        yc = y - mean
        var = _seg_sum(yc * yc, seg) * (1.0 / DH_B)
        yield
        yn = yc * lax.rsqrt(var + GN_EPS_B) * ln_w + ln_b
        bonus = _seg_sum(r_s[rs, :] * k_s[rs, :] * r_k, seg) * v_s[rs, :]
        yield
        o_ref[0, rs, :] = ((yn + bonus) * g_s[rs, :]).astype(o_ref.dtype)

    for step in range(nsb + 2):
        stages = []
        if step < nsb:
            stages.append(block(step))
        if 1 <= step <= nsb:
            stages.append(scan(step - 1))
        if step >= 2:
            stages.append(finish(step - 2))
        _interleave(stages)
    pbuf[0:CARRY, :] = pbuf[rows:rows + CARRY, :]


def _mamba_kernel(nc, pad_rows, h_ref, npre_ref, w_ref, cw_ref, cb_ref, hp_ref, vp_ref, o_ref,
                  hn_s, ubuf, z_s, x_s, bm_s, cm_s, dt_s, a_s, e_in_s, st_ref):
    rows = nc * CHUNK
    t_idx = pl.program_id(1)

    @pl.when(t_idx == 0)
    def _():
        ubuf[0:CARRY, :] = jnp.zeros((CARRY, WC_CONV), F32)
        st_ref[...] = jnp.zeros(st_ref.shape, F32)

    n_conv = WC_CONV // PW
    n_z = D_CP // PW

    def front(sb):
        r0 = sb * SBR
        rs = slice(r0, r0 + SBR)
        hn, valid = _normed_input(h_ref, npre_ref, t_idx, rows, pad_rows, r0, SBR)
        hn_s[rs, :] = hn
        yield
        for j in range(n_conv + n_z + 1):
            if j < n_conv:
                cols = slice(D_CP + j * PW, D_CP + (j + 1) * PW)
                ubuf[CARRY + r0:CARRY + r0 + SBR, j * PW:(j + 1) * PW] = jnp.dot(
                    hn_s[rs, :], w_ref[:, cols], preferred_element_type=F32)
                for blk_i in range(j * PW // LANES, (j + 1) * PW // LANES):
                    lo = blk_i * LANES
                    y = _silu(_causal_conv_block(ubuf, cw_ref, lo, SBR, r0)
                              + cb_ref[:, lo:lo + LANES])
                    if lo < D_CP:
                        x_s[rs, lo:lo + LANES] = y
                    elif lo < D_CP + G_C * N_C:
                        bm_s[rs, lo - D_CP:lo - D_CP + LANES] = jnp.where(valid, y, 0.0)
                    else:
                        o2 = lo - D_CP - G_C * N_C
                        cm_s[rs, o2:o2 + LANES] = jnp.where(valid, y, 0.0)
            elif j < n_conv + n_z:
                cols = slice((j - n_conv) * PW, (j - n_conv + 1) * PW)
                z_s[rs, cols] = jnp.dot(hn_s[rs, :], w_ref[:, cols], preferred_element_type=F32)
            else:
                dtp = jnp.dot(hn_s[rs, :], w_ref[:, D_CP + WC_CONV:WC],
                              preferred_element_type=F32)
                dt = jnp.where(valid, _softplus(dtp + hp_ref[0:1, :]), 0.0)
                dt_s[rs, :] = dt
                a_s[rs, :] = -jnp.exp(hp_ref[1:2, :]) * dt
            yield

    incl, _, tri = _tri_masks(SBR)
    glane = _iota2((SBR, GW), 1)
    lane = _iota2((SBR, LANES), 1)
    prow = _iota2((GW, N_C), 0)
    d_skip = vp_ref[0:1, :]

    def expand_heads(f):
        outs = []
        for b in range(D_CP // LANES):
            gi, hb = divmod(b, GW // LANES)
            h0 = gi * HPG + 2 * hb
            second = f[:, h0 + 1:h0 + 2] if hb == 0 else 0.0
            outs.append(jnp.where(lane < DH_C, f[:, h0:h0 + 1], second))
        return jnp.concatenate(outs, axis=1)

    groups = range(G_C)
    gs = [slice(gi * GW, (gi + 1) * GW) for gi in groups]
    ns = [slice(gi * N_C, (gi + 1) * N_C) for gi in groups]

    def block(sb):
        rs = _rows(sb, SBR)
        a_sb = a_s[rs, :]
        acs = _dot_lx3(tri, a_sb)
        a_end = _chunk_last(acs)
        dt_sb = dt_s[rs, :]
        acst = acs.T
        dtt = dt_sb.T
        yield
        e_in_s[rs, :] = expand_heads(jnp.exp(acs))
        a_s[rs, :] = jnp.exp(a_end)
        xs = x_s[rs, :]
        x_s[rs, :] = xs * expand_heads(dt_sb * jnp.exp(a_end - acs))
        yield
        in_head = [(glane >= hh * DH_C) & (glane < (hh + 1) * DH_C) for hh in range(HPG)]
        for first in range(0, G_C, MAMBA_GROUP):
            part = range(first, first + MAMBA_GROUP)
            cb = {gi: _dot_nt(cm_s[rs, ns[gi]], bm_s[rs, ns[gi]]) for gi in part}
            yield
            m = {}
            for gi in part:
                for hd in range(gi * HPG, (gi + 1) * HPG):
                    decay = jnp.exp(jnp.where(incl, acs[:, hd:hd + 1] - acst[hd:hd + 1, :], 0.0))
                    m[hd] = jnp.where(incl, cb[gi] * decay, 0.0) * dtt[hd:hd + 1, :]
                yield
            terms = {hd: _dot(m[hd], jnp.where(in_head[hd % HPG], xs[:, gs[hd // HPG]], 0.0))
                     for hd in m}
            yield
            for gi in part:
                y = d_skip[:, gs[gi]] * xs[:, gs[gi]]
                for hh in range(HPG):
                    y = y + terms[gi * HPG + hh]
                o_ref[0, rs, gs[gi]] = y

    def scan(sb):
        for c in range(sb * SB_CHUNKS, (sb + 1) * SB_CHUNKS):
            rs = _rows(c, CHUNK)
            state = [st_ref[gi] for gi in groups]
            y_off = [_dot_nt(cm_s[rs, ns[gi]], state[gi]) for gi in groups]
            upd = [_dot_tn(x_s[rs, gs[gi]], bm_s[rs, ns[gi]]) for gi in groups]
            yield
            for gi in groups:
                o_ref[0, rs, gs[gi]] = o_ref[0, rs, gs[gi]] + y_off[gi] * e_in_s[rs, gs[gi]]
                scale_rows = jnp.zeros((GW, N_C), F32)
                for hh in range(HPG):
                    hd = gi * HPG + hh
                    scale_rows = jnp.where((prow >= hh * DH_C) & (prow < (hh + 1) * DH_C),
                                           a_s[_row(c, CHUNK), hd:hd + 1], scale_rows)
                st_ref[gi] = state[gi] * scale_rows + upd[gi]
            yield

    def finish(sb):
        rs = _rows(sb, SBR)
        for gi in groups:
            y = o_ref[0, rs, gs[gi]] * _silu(z_s[rs, gs[gi]])
            ms = jnp.sum(y * y, axis=-1, keepdims=True) * (1.0 / (HPG * DH_C))
            o_ref[0, rs, gs[gi]] = y * lax.rsqrt(ms + NORM_EPS) * vp_ref[1:2, gs[gi]]
            yield

    nsb = nc // SB_CHUNKS
    _drain(front(0))
    for step in range(nsb + 2):
        stages = []
        if step < nsb:
            stages.append(block(step))
        if step + 1 < nsb:
            stages.append(front(step + 1))
        if 1 <= step <= nsb:
            stages.append(scan(step - 1))
        if step >= 2:
            stages.append(finish(step - 2))
        _interleave(stages)
    ubuf[0:CARRY, :] = ubuf[rows:rows + CARRY, :]


def _out_kernel(h_ref, oa_ref, ob_ref, oc_ref, wa_ref, wb_ref, wc_ref, npost_ref, o_ref):
    out = (_dot(oa_ref[0], wa_ref[...]) + _dot(ob_ref[0], wb_ref[...])
           + _dot(oc_ref[0], wc_ref[...]))
    ms = jnp.mean(out * out, axis=-1, keepdims=True)
    o_ref[0] = h_ref[0] + out * lax.rsqrt(ms + NORM_EPS) * npost_ref[...]


def _pad_lanes(x, width=LANES):
    return jnp.pad(x, [(0, 0)] * (x.ndim - 1) + [(0, width - x.shape[-1])])


def _group_pad(x):
    lead = x.shape[:-1]
    xg = x.reshape(lead + (G_C, HPG * DH_C))
    xg = jnp.pad(xg, [(0, 0)] * len(lead) + [(0, 0), (0, GW - HPG * DH_C)])
    return xg.reshape(lead + (D_CP,))


def _tiling(seq):
    n_sb = -(-(N_META + seq) // SBR)
    best = None
    for nsb in range(MAX_TILE_SB, 0, -1):
        total = n_sb + (-n_sb) % nsb
        if best is None or total < best[1]:
            best = (nsb, total)
    nsb, total = best
    lp = total * SBR
    return nsb, lp - N_META - seq, lp


def _const_spec(shape):
    return pl.BlockSpec(shape, lambda b, t: (0,) * len(shape))


def _mixer_call(body, rows, bsz, lp, out_w, out_dtype, consts, scratch, name, h):
    tile = pl.BlockSpec((1, rows, D_MODEL), lambda b, t: (b, t, 0))
    return pl.pallas_call(
        body,
        out_shape=jax.ShapeDtypeStruct((bsz, lp, out_w), out_dtype),
        grid=(bsz, lp // rows),
        in_specs=[tile] + [_const_spec(c.shape) for c in consts],
        out_specs=pl.BlockSpec((1, rows, out_w), lambda b, t: (b, t, 0)),
        scratch_shapes=scratch,
        compiler_params=pltpu.CompilerParams(
            dimension_semantics=("arbitrary", "arbitrary"), vmem_limit_bytes=VMEM_LIMIT),
        name=name,
    )(h, *consts)


def kernel(x, meta_tokens, norm_pre, norm_post, w_in, w_out, dn_conv, dn_A_log, dn_dt_bias, dn_norm, rw_mu, rw_w0, rw_w2, rw_a0, rw_a2, rw_k_k, rw_k_a, rw_r_k, rw_ln_w, rw_ln_b, mb_conv, mb_conv_b, mb_dt_bias, mb_A_log, mb_D, mb_norm):
    bsz, seq, _ = x.shape
    depth = w_in.shape[0]
    nsb, pad_rows, lp = _tiling(seq)
    rows = nsb * SBR
    nc = nsb * SB_CHUNKS
    meta = jnp.broadcast_to(meta_tokens.astype(x.dtype)[None], (bsz, N_META, D_MODEL))
    h = jnp.concatenate([jnp.zeros((bsz, pad_rows, D_MODEL), x.dtype), meta, x], axis=1)

    vm = lambda r, c: pltpu.VMEM((r, c), F32)
    hn_scratch = pltpu.VMEM((rows, D_MODEL), BF16)
    for l in range(depth):
        npre = norm_pre[l][None, :]
        wl = w_in[l]
        w_a = jnp.concatenate([wl[:, :4 * D_A], _pad_lanes(wl[:, 4 * D_A:4 * D_A + H_A]),
                               _pad_lanes(wl[:, 4 * D_A + H_A:P_A])], axis=1).astype(BF16)
        hp_a = jnp.stack([_pad_lanes(dn_A_log[l]), _pad_lanes(dn_dt_bias[l])])
        hp_a = jnp.pad(hp_a, ((0, 6), (0, 0)))
        o_a = _mixer_call(
            functools.partial(_deltanet_kernel, nsb, pad_rows), rows, bsz, lp, D_A, BF16,
            [npre, w_a, dn_conv[l], hp_a, dn_norm[l][None, :]],
            [hn_scratch, vm(rows + CARRY, 3 * D_A), vm(rows, D_A), vm(rows, D_A), vm(rows, D_A),
             vm(rows, D_A), vm(rows, 2 * LANES), vm(rows, LANES), vm(rows, D_A), vm(rows, D_A),
             pltpu.VMEM((H_A, DH_A, DH_A), F32)],
            "deltanet", h)
        wb_l = wl[:, P_A:P_A + P_B]
        w_b = jnp.concatenate([wb_l[:, :4 * D_B], _pad_lanes(wb_l[:, 4 * D_B:4 * D_B + R_W]),
                               _pad_lanes(wb_l[:, 4 * D_B + R_W:])], axis=1).astype(BF16)
        mu = rw_mu[l]
        mu_b = jnp.concatenate([mu[:4 * D_B], _pad_lanes(mu[4 * D_B:4 * D_B + R_W]),
                                _pad_lanes(mu[4 * D_B + R_W:])])[None, :]
        w2p = jnp.pad(rw_w2[l], ((0, LANES - R_W), (0, 0))).astype(BF16)
        a2p = jnp.pad(rw_a2[l], ((0, LANES - R_A), (0, 0))).astype(BF16)
        vp_b = jnp.stack([rw_w0[l], rw_a0[l], rw_k_k[l], rw_k_a[l], rw_r_k[l].reshape(D_B),
                          rw_ln_w[l], rw_ln_b[l], jnp.zeros((D_B,), F32)])
        o_b = _mixer_call(
            functools.partial(_rwkv_kernel, nsb, pad_rows), rows, bsz, lp, D_B, BF16,
            [npre, w_b, mu_b, w2p, a2p, vp_b],
            [hn_scratch, vm(rows + CARRY, WB)] + [vm(rows, D_B)] * 11
            + [pltpu.VMEM((H_B // 2, LANES, LANES), F32)],
            "rwkv7", h)
        wc_l = wl[:, P_A + P_B:]
        w_c = jnp.concatenate([
            _group_pad(wc_l[:, :D_C]), _group_pad(wc_l[:, D_C:2 * D_C]),
            wc_l[:, 2 * D_C:2 * D_C + 2 * G_C * N_C], _pad_lanes(wc_l[:, 2 * D_C + 2 * G_C * N_C:]),
        ], axis=1).astype(BF16)
        cw_c = jnp.concatenate([_group_pad(mb_conv[l][:, :D_C]), mb_conv[l][:, D_C:]], axis=1)
        cb_c = jnp.concatenate([_group_pad(mb_conv_b[l][:D_C]), mb_conv_b[l][D_C:]])[None, :]
        hp_c = jnp.pad(jnp.stack([_pad_lanes(mb_dt_bias[l]), _pad_lanes(mb_A_log[l])]),
                       ((0, 6), (0, 0)))
        vp_c = jnp.pad(jnp.stack([_group_pad(jnp.repeat(mb_D[l], DH_C)), _group_pad(mb_norm[l])]),
                       ((0, 6), (0, 0)))
        o_c = _mixer_call(
            functools.partial(_mamba_kernel, nc, pad_rows), rows, bsz, lp, D_CP, F32,
            [npre, w_c, cw_c, cb_c, hp_c, vp_c],
            [hn_scratch, vm(rows + CARRY, WC_CONV), vm(rows, D_CP), vm(rows, D_CP),
             vm(rows, G_C * N_C),
             vm(rows, G_C * N_C), vm(rows, LANES), vm(rows, LANES), vm(rows, D_CP),
             pltpu.VMEM((G_C, GW, N_C), F32)],
            "mamba2", h)
        wo = w_out[l]
        wo_a = wo[:D_A].astype(BF16)
        wo_b = wo[D_A:D_A + D_B].astype(BF16)
        wo_c = jnp.pad(wo[D_A + D_B:].reshape(G_C, HPG * DH_C, D_MODEL),
                       ((0, 0), (0, GW - HPG * DH_C), (0, 0))).reshape(D_CP, D_MODEL).astype(BF16)
        row_spec = lambda w: pl.BlockSpec((1, rows, w), lambda b, t: (b, t, 0))
        h = pl.pallas_call(
            _out_kernel,
            out_shape=jax.ShapeDtypeStruct((bsz, lp, D_MODEL), F32),
            grid=(bsz, lp // rows),
            in_specs=[row_spec(D_MODEL), row_spec(D_A), row_spec(D_B), row_spec(D_CP),
                      _const_spec(wo_a.shape), _const_spec(wo_b.shape), _const_spec(wo_c.shape),
                      _const_spec((1, D_MODEL))],
            out_specs=row_spec(D_MODEL),
            compiler_params=pltpu.CompilerParams(
                dimension_semantics=("arbitrary", "arbitrary"), vmem_limit_bytes=VMEM_LIMIT),
            name="out_proj",
        )(h, o_a, o_b, o_c, wo_a, wo_b, wo_c, norm_post[l][None, :])
    return h[:, pad_rows + N_META:]
```

```python
import functools

import jax
import jax.numpy as jnp
from jax import lax
from jax.experimental import pallas as pl
from jax.experimental.pallas import tpu as pltpu

F32 = jnp.float32
BF16 = jnp.bfloat16

D_MODEL = 1024
CHUNK = 64
SB_CHUNKS = 4
SBR = SB_CHUNKS * CHUNK
N_META = 16
CONV_K = 4
NORM_EPS = 1e-6
LANES = 128
CARRY = 8

H_A, DH_A = 6, 128
D_A = H_A * DH_A
H_B, DH_B = 8, 64
D_B = H_B * DH_B
R_W = R_A = 64
GN_EPS_B = 64e-5
H_C, DH_C = 12, 64
D_C = H_C * DH_C
G_C, N_C = 4, 128
HPG = H_C // G_C
GW = 256
D_CP = G_C * GW
P_A = 4 * D_A + 2 * H_A
P_B = 4 * D_B + R_W + R_A
P_C = 2 * D_C + 2 * G_C * N_C + H_C

WA = 4 * D_A + 2 * LANES
WB = 4 * D_B + 2 * LANES
WC_CONV = D_CP + 2 * G_C * N_C
WC = D_CP + WC_CONV + LANES

VMEM_LIMIT = 56 * 1024 * 1024
MAX_TILE_SB = 3
PW = 256
DELTA_GROUP = 3
RWKV_GROUP = 8
MAMBA_GROUP = 2


def _dot(a, b):
    return jnp.dot(a.astype(BF16), b.astype(BF16), preferred_element_type=F32)


def _dot_nt(a, b):
    return lax.dot_general(a.astype(BF16), b.astype(BF16), (((1,), (1,)), ((), ())),
                           preferred_element_type=F32)


def _dot_tn(a, b):
    return lax.dot_general(a.astype(BF16), b.astype(BF16), (((0,), (0,)), ((), ())),
                           preferred_element_type=F32)


def _split2(x):
    hi = x.astype(BF16)
    lo = (x - hi.astype(F32)).astype(BF16)
    return hi, lo


def _split3(x):
    hi = x.astype(BF16)
    r = x - hi.astype(F32)
    mid = r.astype(BF16)
    lo = (r - mid.astype(F32)).astype(BF16)
    return hi, mid, lo


def _dot_lx3(l_bf16, x):
    hi, mid, lo = _split3(x)
    d = lambda p: jnp.dot(l_bf16, p, preferred_element_type=F32)
    return d(hi) + d(mid) + d(lo)


def _dot_x2r(x, r_bf16):
    hi, lo = _split2(x)
    d = lambda p: jnp.dot(p, r_bf16, preferred_element_type=F32)
    return d(hi) + d(lo)


def _dot_x3r(x, r_bf16):
    hi, mid, lo = _split3(x)
    d = lambda p: jnp.dot(p, r_bf16, preferred_element_type=F32)
    return d(hi) + d(mid) + d(lo)


def _to_wide(m, lane_blk):
    s = [m[c * CHUNK:(c + 1) * CHUNK, :] for c in range(SB_CHUNKS)]
    out = s[SB_CHUNKS - 1]
    for c in range(SB_CHUNKS - 2, -1, -1):
        out = jnp.where(lane_blk == c, s[c], out)
    return out


def _to_block_diag(w, lane_blk):
    wb = w.astype(BF16)
    zero = jnp.zeros_like(wb)
    return jnp.concatenate([jnp.where(lane_blk == c, wb, zero) for c in range(SB_CHUNKS)], axis=0)


def _neumann_inverses(xs, eye_w, lane_blk):
    bd = lambda w: _to_block_diag(w, lane_blk)
    mm = lambda a, b: jnp.dot(a.astype(BF16), b, preferred_element_type=F32)
    xw = [_to_wide(x, lane_blk) for x in xs]
    p = [eye_w + w for w in xw]
    xk = [mm(w, bd(w)) for w in xw]
    yield
    for _ in range(4):
        both = [mm(a, jnp.concatenate([bd(a), bd(b)], axis=1)) for a, b in zip(xk, p)]
        xk = [t[:, 0:SBR] for t in both]
        p = [b + t[:, SBR:2 * SBR] for b, t in zip(p, both)]
        yield
    p = [b + mm(a, bd(b)) for a, b in zip(xk, p)]
    return [bd(b) for b in p]


def _drain(gen):
    try:
        while True:
            next(gen)
    except StopIteration as stop:
        return stop.value


def _interleave(gens):
    live = list(gens)
    while live:
        for g in list(live):
            try:
                next(g)
            except StopIteration:
                live.remove(g)


def _rows(i, n):
    return slice(i * n, (i + 1) * n)


def _row(i, n):
    return slice(i * n, i * n + 1)


def _silu(x):
    return x * jax.nn.sigmoid(x)


def _softplus(x):
    return jnp.maximum(x, 0.0) + jnp.log1p(jnp.exp(-jnp.abs(x)))


def _iota2(shape, dim):
    return lax.broadcasted_iota(jnp.int32, shape, dim)


def _tri_masks(n):
    r = _iota2((n, n), 0)
    c = _iota2((n, n), 1)
    same = (r // CHUNK) == (c // CHUNK)
    incl = same & (r >= c)
    strict = same & (r > c)
    tri = jnp.where(incl, 1.0, 0.0).astype(BF16)
    return incl, strict, tri


def _wide_consts():
    r = _iota2((CHUNK, SBR), 0)
    c = _iota2((CHUNK, SBR), 1)
    return jnp.where(c % CHUNK == r, 1.0, 0.0).astype(F32), c // CHUNK


def _normed_input(h_ref, npre_ref, t_idx, rows, pad_rows, r0=0, n=None):
    n = rows if n is None else n
    h = h_ref[0, r0:r0 + n, :]
    ms = jnp.mean(h * h, axis=-1, keepdims=True)
    hn = h * lax.rsqrt(ms + NORM_EPS) * npre_ref[...]
    gid = t_idx * rows + r0 + _iota2((n, 1), 0)
    valid = gid >= pad_rows
    hn = jnp.where(valid, hn, 0.0)
    return hn.astype(BF16), valid


def _chunk_last(x):
    return jnp.concatenate(
        [jnp.broadcast_to(x[(c + 1) * CHUNK - 1:(c + 1) * CHUNK, :], (CHUNK, x.shape[1]))
         for c in range(SB_CHUNKS)], axis=0)


def _causal_conv_block(ubuf, cw_ref, lo, rows, r0=0):
    full = ubuf[r0:r0 + CARRY + rows, lo:lo + LANES]
    tap = lambda j: cw_ref[j:j + 1, lo:lo + LANES]
    prev = pltpu.roll(full, 1, axis=0)
    near = full * tap(3) + prev * tap(2)
    far = full * tap(1) + prev * tap(0)
    return (near + pltpu.roll(far, 2, axis=0))[CARRY:]


def _deltanet_kernel(nsb, pad_rows, h_ref, npre_ref, w_ref, cw_ref, hp_ref, nw_ref, o_ref,
                     hn_s, ubuf, q_s, k_s, v_s, z_s, gb_s, gl_s, u_s, w_s, st_ref):
    rows = nsb * SBR
    t_idx = pl.program_id(1)

    @pl.when(t_idx == 0)
    def _():
        ubuf[0:CARRY, :] = jnp.zeros((CARRY, 3 * D_A), F32)
        st_ref[...] = jnp.zeros(st_ref.shape, F32)

    n_conv = 3 * D_A // PW
    n_z = D_A // PW
    incl, strict, tri = _tri_masks(SBR)
    eye_w, lane_blk = _wide_consts()
    scale = DH_A ** -0.5
    heads = range(H_A)
    hs = [slice(hd * DH_A, (hd + 1) * DH_A) for hd in heads]

    def front(sb):
        r0 = sb * SBR
        rs = slice(r0, r0 + SBR)
        hn, valid = _normed_input(h_ref, npre_ref, t_idx, rows, pad_rows, r0, SBR)
        hn_s[rs, :] = hn
        yield
        for j in range(n_conv + n_z + 1):
            cols = slice(j * PW, (j + 1) * PW) if j < n_conv + n_z else slice(4 * D_A, WA)
            p = jnp.dot(hn_s[rs, :], w_ref[:, cols], preferred_element_type=F32)
            if j < n_conv:
                ubuf[CARRY + r0:CARRY + r0 + SBR, cols] = p
                for blk_i in range(j * PW // LANES, (j + 1) * PW // LANES):
                    y = _silu(_causal_conv_block(ubuf, cw_ref, blk_i * LANES, SBR, r0))
                    if blk_i < 2 * H_A:
                        y = y * lax.rsqrt(jnp.sum(y * y, axis=-1, keepdims=True) + 1e-6)
                    dst = (q_s, k_s, v_s)[blk_i // H_A]
                    dst[rs, hs[blk_i % H_A]] = y
            elif j < n_conv + n_z:
                z_s[rs, (j - n_conv) * PW:(j - n_conv + 1) * PW] = p
            else:
                g = -jnp.exp(hp_ref[0:1, :]) * _softplus(p[:, 0:LANES] + hp_ref[1:2, :])
                gb_s[rs, 0:LANES] = jnp.where(valid, g, 0.0)
                gb_s[rs, LANES:2 * LANES] = jnp.where(
                    valid, jax.nn.sigmoid(p[:, LANES:2 * LANES]), 0.0)
            yield

    def block(sb):
        rs = slice(sb * SBR, (sb + 1) * SBR)
        g_sb = gb_s[rs, 0:LANES]
        gc = _dot_lx3(tri, g_sb)
        gl = _chunk_last(gc)
        gl_s[rs, :] = gl
        gct = gc.T
        beta_sb = gb_s[rs, LANES:2 * LANES]
        yield
        for first in range(0, H_A, DELTA_GROUP):
            yield from block_heads(rs, range(first, first + DELTA_GROUP), gc, gct, gl, beta_sb)

    def block_heads(rs, grp, gc, gct, gl, beta_sb):
        k = {hd: k_s[rs, hs[hd]] for hd in grp}
        gcol = {hd: gc[:, hd:hd + 1] for hd in grp}
        bcol = {hd: beta_sb[:, hd:hd + 1] for hd in grp}
        dmask = {hd: jnp.where(incl, jnp.exp(jnp.where(incl, gcol[hd] - gct[hd:hd + 1, :], 0.0)), 0.0)
                 for hd in grp}
        yield
        kb = {hd: k[hd] * bcol[hd] for hd in grp}
        a_neg = [jnp.where(strict, -(_dot_nt(kb[hd], k[hd]) * dmask[hd]), 0.0) for hd in grp]
        yield
        t_inv = dict(zip(grp, (yield from _neumann_inverses(a_neg, eye_w, lane_blk))))
        yield
        eg = {hd: jnp.exp(gcol[hd]) for hd in grp}
        sol = {hd: _dot(t_inv[hd], jnp.concatenate([v_s[rs, hs[hd]] * bcol[hd], kb[hd] * eg[hd]],
                                                   axis=1)) for hd in grp}
        yield
        qs = {hd: q_s[rs, hs[hd]] * scale for hd in grp}
        attn = {hd: _dot_nt(qs[hd], k[hd]) * dmask[hd] for hd in grp}
        yield
        asol = {hd: _dot(attn[hd], sol[hd]) for hd in grp}
        yield
        for hd in grp:
            u_s[rs, hs[hd]] = sol[hd][:, 0:DH_A]
            w_s[rs, hs[hd]] = sol[hd][:, DH_A:2 * DH_A]
            v_s[rs, hs[hd]] = asol[hd][:, 0:DH_A]
            q_s[rs, hs[hd]] = qs[hd] * eg[hd] - asol[hd][:, DH_A:2 * DH_A]
            k_s[rs, hs[hd]] = k[hd] * jnp.exp(gl[:, hd:hd + 1] - gcol[hd])

    def scan(sb):
        for c in range(sb * SB_CHUNKS, (sb + 1) * SB_CHUNKS):
            rs = _rows(c, CHUNK)
            state = [st_ref[hd] for hd in heads]
            res = [_dot(jnp.concatenate([w_s[rs, hs[hd]], q_s[rs, hs[hd]]], axis=0), state[hd])
                   for hd in heads]
            yield
            v_new = [u_s[rs, hs[hd]] - res[hd][0:CHUNK] for hd in heads]
            upd = [_dot_tn(k_s[rs, hs[hd]], v_new[hd]) for hd in heads]
            yield
            for hd in heads:
                v_s[rs, hs[hd]] = v_s[rs, hs[hd]] + res[hd][CHUNK:2 * CHUNK]
                g_tot = jnp.exp(gl_s[_row(c, CHUNK), hd:hd + 1])
                st_ref[hd] = state[hd] * g_tot + upd[hd]
            yield

    def finish(sb):
        rs = slice(sb * SBR, (sb + 1) * SBR)
        for hd in heads:
            o = v_s[rs, hs[hd]]
            o = o * lax.rsqrt(jnp.mean(o * o, axis=-1, keepdims=True) + NORM_EPS) * nw_ref[...]
            o_ref[0, rs, hs[hd]] = (o * _silu(z_s[rs, hs[hd]])).astype(o_ref.dtype)
            yield

    _drain(front(0))
    for step in range(nsb + 2):
        stages = []
        if step < nsb:
            stages.append(block(step))
        if step + 1 < nsb:
            stages.append(front(step + 1))
        if 1 <= step <= nsb:
            stages.append(scan(step - 1))
        if step >= 2:
            stages.append(finish(step - 2))
        _interleave(stages)
    ubuf[0:CARRY, :] = ubuf[rows:rows + CARRY, :]


def _seg_sum(x, seg):
    width = seg.shape[0]
    outs = []
    for b in range(x.shape[1] // width):
        outs.append(_dot_x2r(x[:, b * width:(b + 1) * width], seg))
    return jnp.concatenate(outs, axis=1)


def _rwkv_kernel(nsb, pad_rows, h_ref, npre_ref, w_ref, mu_ref, w2_ref, a2_ref, vp_ref, o_ref,
                 hn_s, pbuf, r_s, k_s, v_s, a_s, b_s, lw_s, g_s, y_s, u0_s, be_s, ke_s, st_ref):
    rows = nsb * SBR
    t_idx = pl.program_id(1)

    @pl.when(t_idx == 0)
    def _():
        pbuf[0:CARRY, :] = jnp.zeros((CARRY, WB), F32)
        st_ref[...] = jnp.zeros(st_ref.shape, F32)

    def mixed(r0, n, lo, width):
        full = pbuf[r0:r0 + CARRY + n, lo:lo + width]
        cur = full[CARRY:]
        prev = pltpu.roll(full, 1, axis=0)[CARRY:]
        return cur + (prev - cur) * mu_ref[:, lo:lo + width]

    w0 = vp_ref[0:1, :]
    a0 = vp_ref[1:2, :]
    k_k = vp_ref[2:3, :]
    k_a = vp_ref[3:4, :]
    r_k = vp_ref[4:5, :]
    ln_w = vp_ref[5:6, :]
    ln_b = vp_ref[6:7, :]

    hr = _iota2((LANES, LANES), 0) // DH_B
    hc = _iota2((LANES, LANES), 1) // DH_B
    bd = hr == hc
    sr = _iota2((PW, PW), 0) // DH_B
    sc = _iota2((PW, PW), 1) // DH_B
    seg = jnp.where(sr == sc, 1.0, 0.0).astype(BF16)

    col_lo = (4 * D_B, D_B, 0, 2 * D_B, 3 * D_B)
    col_w = (2 * LANES, D_B, D_B, D_B, D_B)

    hn, _ = _normed_input(h_ref, npre_ref, t_idx, rows, pad_rows)
    hn_s[...] = hn

    def project(j):
        cols = slice(col_lo[j], col_lo[j] + col_w[j])
        pbuf[CARRY:CARRY + rows, cols] = jnp.dot(hn_s[...], w_ref[:, cols],
                                                 preferred_element_type=F32)

    def activate(j):
        if j == 0:
            w_lo = mixed(0, rows, 4 * D_B, LANES)
            a_lo = mixed(0, rows, 4 * D_B + LANES, LANES)
            lw_s[...] = -jnp.exp(-_softplus(-(w0 + _dot(jnp.tanh(w_lo), w2_ref[...]))) - 0.5)
            u0_s[...] = jax.nn.sigmoid(a0 + _dot(a_lo, a2_ref[...]))
        elif j == 1:
            k = mixed(0, rows, D_B, D_B)
            a_lr = u0_s[...]
            kk = k * k_k
            kk = kk * lax.rsqrt(_seg_sum(kk * kk, seg) + 1e-6)
            k_s[...] = k * (1.0 + (a_lr - 1.0) * k_a)
            a_s[...] = -kk
            b_s[...] = kk * a_lr
        elif j == 2:
            r_s[...] = mixed(0, rows, 0, D_B)
        elif j == 3:
            v_s[...] = mixed(0, rows, 2 * D_B, D_B)
        else:
            g_s[...] = _silu(mixed(0, rows, 3 * D_B, D_B))

    project(0)
    for j in range(len(col_lo)):
        if j + 1 < len(col_lo):
            project(j + 1)
        activate(j)

    incl, strict, tri = _tri_masks(SBR)
    eye_w, lane_blk = _wide_consts()
    lane = _iota2((SBR, LANES), 1)
    half = (lane < DH_B, lane >= DH_B)

    def block(sb):
        rs = _rows(sb, SBR)
        lw = lw_s[rs, :]
        cl = _dot_lx3(tri, lw)
        cl_end = _chunk_last(cl)
        e_neg = jnp.exp(-cl)
        e_end = jnp.exp(cl_end - cl)
        a_t = a_s[rs, :] * jnp.exp(cl - lw)
        r_t = r_s[rs, :] * jnp.exp(cl)
        b_c = b_s[rs, :]
        k_c = k_s[rs, :]
        v_c = v_s[rs, :]
        b_t = b_c * e_neg
        k_t = k_c * e_neg
        be_s[rs, :] = b_c * e_end
        ke_s[rs, :] = k_c * e_end
        lw_s[rs, :] = jnp.exp(cl_end)
        yield
        for first in range(0, H_B, RWKV_GROUP):
            yield from block_heads(rs, range(first, first + RWKV_GROUP), a_t, r_t, b_t, k_t, v_c)

    def block_heads(rs, heads, a_t, r_t, b_t, k_t, v_c):
        ls = {hd: slice((hd // 2) * LANES, (hd // 2 + 1) * LANES) for hd in heads}
        am = {hd: jnp.where(half[hd % 2], a_t[:, ls[hd]], 0.0) for hd in heads}
        rm = {hd: jnp.where(half[hd % 2], r_t[:, ls[hd]], 0.0) for hd in heads}
        quad = {hd: _dot_nt(jnp.concatenate([am[hd], rm[hd]], axis=0),
                            jnp.concatenate([b_t[:, ls[hd]], k_t[:, ls[hd]]], axis=0))
                for hd in heads}
        yield
        ab = [jnp.where(strict, quad[hd][0:SBR, 0:SBR], 0.0) for hd in heads]
        t_inv = dict(zip(heads, (yield from _neumann_inverses(ab, eye_w, lane_blk))))
        yield
        akv = {hd: _dot(jnp.where(strict, quad[hd][0:SBR, SBR:2 * SBR], 0.0), v_c[:, ls[hd]])
               for hd in heads}
        yield
        sol = {hd: _dot(t_inv[hd], jnp.concatenate([am[hd], akv[hd]], axis=1)) for hd in heads}
        yield
        rbsol = {hd: _dot(jnp.where(incl, quad[hd][SBR:2 * SBR, 0:SBR], 0.0), sol[hd])
                 for hd in heads}
        rkv = {hd: _dot(jnp.where(incl, quad[hd][SBR:2 * SBR, SBR:2 * SBR], 0.0), v_c[:, ls[hd]])
               for hd in heads}
        yield
        for h0 in list(heads)[0::2]:
            h1 = h0 + 1
            a_s[rs, ls[h0]] = sol[h0][:, 0:LANES] + sol[h1][:, 0:LANES]
            b_s[rs, ls[h0]] = rm[h0] + rbsol[h0][:, 0:LANES] + rm[h1] + rbsol[h1][:, 0:LANES]
            u0_s[rs, ls[h0]] = jnp.where(half[0], sol[h0][:, LANES:2 * LANES],
                                         sol[h1][:, LANES:2 * LANES])
            y_s[rs, ls[h0]] = jnp.where(half[0], rbsol[h0][:, LANES:2 * LANES] + rkv[h0],
                                        rbsol[h1][:, LANES:2 * LANES] + rkv[h1])

    def scan(sb):
        pairs = range(H_B // 2)
        ls = [slice(pr * LANES, (pr + 1) * LANES) for pr in pairs]
        for c in range(sb * SB_CHUNKS, (sb + 1) * SB_CHUNKS):
            rs = _rows(c, CHUNK)
            state = [st_ref[pr] for pr in pairs]
            res = [_dot_nt(jnp.concatenate([a_s[rs, ls[pr]], b_s[rs, ls[pr]]], axis=0), state[pr])
                   for pr in pairs]
            yield
            u = [res[pr][0:CHUNK] + u0_s[rs, ls[pr]] for pr in pairs]
            upd = [_dot_tn(jnp.concatenate([u[pr], v_s[rs, ls[pr]]], axis=0),
                           jnp.concatenate([be_s[rs, ls[pr]], ke_s[rs, ls[pr]]], axis=0))
                   for pr in pairs]
            yield
            for pr in pairs:
                y_s[rs, ls[pr]] = y_s[rs, ls[pr]] + res[pr][CHUNK:2 * CHUNK]
                gam = lw_s[_row(c, CHUNK), ls[pr]]
                st_ref[pr] = jnp.where(bd, state[pr] * gam + upd[pr], 0.0)
            yield

    def finish(sb):
        rs = _rows(sb, SBR)
        y = y_s[rs, :]
        mean = _seg_sum(y, seg) * (1.0 / DH_B)
        yield
        yc = y - mean
        var = _seg_sum(yc * yc, seg) * (1.0 / DH_B)
        yield
        yn = yc * lax.rsqrt(var + GN_EPS_B) * ln_w + ln_b
        bonus = _seg_sum(r_s[rs, :] * k_s[rs, :] * r_k, seg) * v_s[rs, :]
        yield
        o_ref[0, rs, :] = ((yn + bonus) * g_s[rs, :]).astype(o_ref.dtype)

    for step in range(nsb + 2):
        stages = []
        if step < nsb:
            stages.append(block(step))
        if 1 <= step <= nsb:
            stages.append(scan(step - 1))
        if step >= 2:
            stages.append(finish(step - 2))
        _interleave(stages)
    pbuf[0:CARRY, :] = pbuf[rows:rows + CARRY, :]


def _mamba_kernel(nc, pad_rows, h_ref, npre_ref, w_ref, cw_ref, cb_ref, hp_ref, vp_ref, o_ref,
                  hn_s, ubuf, z_s, x_s, bm_s, cm_s, dt_s, a_s, e_in_s, st_ref):
    rows = nc * CHUNK
    t_idx = pl.program_id(1)

    @pl.when(t_idx == 0)
    def _():
        ubuf[0:CARRY, :] = jnp.zeros((CARRY, WC_CONV), F32)
        st_ref[...] = jnp.zeros(st_ref.shape, F32)

    n_conv = WC_CONV // PW
    n_z = D_CP // PW

    def front(sb):
        r0 = sb * SBR
        rs = slice(r0, r0 + SBR)
        hn, valid = _normed_input(h_ref, npre_ref, t_idx, rows, pad_rows, r0, SBR)
        hn_s[rs, :] = hn
        yield
        for j in range(n_conv + n_z + 1):
            if j < n_conv:
                cols = slice(D_CP + j * PW, D_CP + (j + 1) * PW)
                ubuf[CARRY + r0:CARRY + r0 + SBR, j * PW:(j + 1) * PW] = jnp.dot(
                    hn_s[rs, :], w_ref[:, cols], preferred_element_type=F32)
                for blk_i in range(j * PW // LANES, (j + 1) * PW // LANES):
                    lo = blk_i * LANES
                    y = _silu(_causal_conv_block(ubuf, cw_ref, lo, SBR, r0)
                              + cb_ref[:, lo:lo + LANES])
                    if lo < D_CP:
                        x_s[rs, lo:lo + LANES] = y
                    elif lo < D_CP + G_C * N_C:
                        bm_s[rs, lo - D_CP:lo - D_CP + LANES] = jnp.where(valid, y, 0.0)
                    else:
                        o2 = lo - D_CP - G_C * N_C
                        cm_s[rs, o2:o2 + LANES] = jnp.where(valid, y, 0.0)
            elif j < n_conv + n_z:
                cols = slice((j - n_conv) * PW, (j - n_conv + 1) * PW)
                z_s[rs, cols] = jnp.dot(hn_s[rs, :], w_ref[:, cols], preferred_element_type=F32)
            else:
                dtp = jnp.dot(hn_s[rs, :], w_ref[:, D_CP + WC_CONV:WC],
                              preferred_element_type=F32)
                dt = jnp.where(valid, _softplus(dtp + hp_ref[0:1, :]), 0.0)
                dt_s[rs, :] = dt
                a_s[rs, :] = -jnp.exp(hp_ref[1:2, :]) * dt
            yield

    incl, _, tri = _tri_masks(SBR)
    glane = _iota2((SBR, GW), 1)
    lane = _iota2((SBR, LANES), 1)
    prow = _iota2((GW, N_C), 0)
    d_skip = vp_ref[0:1, :]

    def expand_heads(f):
        outs = []
        for b in range(D_CP // LANES):
            gi, hb = divmod(b, GW // LANES)
            h0 = gi * HPG + 2 * hb
            second = f[:, h0 + 1:h0 + 2] if hb == 0 else 0.0
            outs.append(jnp.where(lane < DH_C, f[:, h0:h0 + 1], second))
        return jnp.concatenate(outs, axis=1)

    groups = range(G_C)
    gs = [slice(gi * GW, (gi + 1) * GW) for gi in groups]
    ns = [slice(gi * N_C, (gi + 1) * N_C) for gi in groups]

    def block(sb):
        rs = _rows(sb, SBR)
        a_sb = a_s[rs, :]
        acs = _dot_lx3(tri, a_sb)
        a_end = _chunk_last(acs)
        dt_sb = dt_s[rs, :]
        acst = acs.T
        dtt = dt_sb.T
        yield
        e_in_s[rs, :] = expand_heads(jnp.exp(acs))
        a_s[rs, :] = jnp.exp(a_end)
        xs = x_s[rs, :]
        x_s[rs, :] = xs * expand_heads(dt_sb * jnp.exp(a_end - acs))
        yield
        in_head = [(glane >= hh * DH_C) & (glane < (hh + 1) * DH_C) for hh in range(HPG)]
        for first in range(0, G_C, MAMBA_GROUP):
            part = range(first, first + MAMBA_GROUP)
            cb = {gi: _dot_nt(cm_s[rs, ns[gi]], bm_s[rs, ns[gi]]) for gi in part}
            yield
            m = {}
            for gi in part:
                for hd in range(gi * HPG, (gi + 1) * HPG):
                    decay = jnp.exp(jnp.where(incl, acs[:, hd:hd + 1] - acst[hd:hd + 1, :], 0.0))
                    m[hd] = jnp.where(incl, cb[gi] * decay, 0.0) * dtt[hd:hd + 1, :]
                yield
            terms = {hd: _dot(m[hd], jnp.where(in_head[hd % HPG], xs[:, gs[hd // HPG]], 0.0))
                     for hd in m}
            yield
            for gi in part:
                y = d_skip[:, gs[gi]] * xs[:, gs[gi]]
                for hh in range(HPG):
                    y = y + terms[gi * HPG + hh]
                o_ref[0, rs, gs[gi]] = y

    def scan(sb):
        for c in range(sb * SB_CHUNKS, (sb + 1) * SB_CHUNKS):
            rs = _rows(c, CHUNK)
            state = [st_ref[gi] for gi in groups]
            y_off = [_dot_nt(cm_s[rs, ns[gi]], state[gi]) for gi in groups]
            upd = [_dot_tn(x_s[rs, gs[gi]], bm_s[rs, ns[gi]]) for gi in groups]
            yield
            for gi in groups:
                o_ref[0, rs, gs[gi]] = o_ref[0, rs, gs[gi]] + y_off[gi] * e_in_s[rs, gs[gi]]
                scale_rows = jnp.zeros((GW, N_C), F32)
                for hh in range(HPG):
                    hd = gi * HPG + hh
                    scale_rows = jnp.where((prow >= hh * DH_C) & (prow < (hh + 1) * DH_C),
                                           a_s[_row(c, CHUNK), hd:hd + 1], scale_rows)
                st_ref[gi] = state[gi] * scale_rows + upd[gi]
            yield

    def finish(sb):
        rs = _rows(sb, SBR)
        for gi in groups:
            y = o_ref[0, rs, gs[gi]] * _silu(z_s[rs, gs[gi]])
            ms = jnp.sum(y * y, axis=-1, keepdims=True) * (1.0 / (HPG * DH_C))
            o_ref[0, rs, gs[gi]] = y * lax.rsqrt(ms + NORM_EPS) * vp_ref[1:2, gs[gi]]
            yield

    nsb = nc // SB_CHUNKS
    _drain(front(0))
    for step in range(nsb + 2):
        stages = []
        if step < nsb:
            stages.append(block(step))
        if step + 1 < nsb:
            stages.append(front(step + 1))
        if 1 <= step <= nsb:
            stages.append(scan(step - 1))
        if step >= 2:
            stages.append(finish(step - 2))
        _interleave(stages)
    ubuf[0:CARRY, :] = ubuf[rows:rows + CARRY, :]


def _out_kernel(h_ref, oa_ref, ob_ref, oc_ref, wa_ref, wb_ref, wc_ref, npost_ref, o_ref):
    out = (_dot(oa_ref[0], wa_ref[...]) + _dot(ob_ref[0], wb_ref[...])
           + _dot(oc_ref[0], wc_ref[...]))
    ms = jnp.mean(out * out, axis=-1, keepdims=True)
    o_ref[0] = h_ref[0] + out * lax.rsqrt(ms + NORM_EPS) * npost_ref[...]


def _pad_lanes(x, width=LANES):
    return jnp.pad(x, [(0, 0)] * (x.ndim - 1) + [(0, width - x.shape[-1])])


def _group_pad(x):
    lead = x.shape[:-1]
    xg = x.reshape(lead + (G_C, HPG * DH_C))
    xg = jnp.pad(xg, [(0, 0)] * len(lead) + [(0, 0), (0, GW - HPG * DH_C)])
    return xg.reshape(lead + (D_CP,))


def _tiling(seq):
    n_sb = -(-(N_META + seq) // SBR)
    best = None
    for nsb in range(MAX_TILE_SB, 0, -1):
        total = n_sb + (-n_sb) % nsb
        if best is None or total < best[1]:
            best = (nsb, total)
    nsb, total = best
    lp = total * SBR
    return nsb, lp - N_META - seq, lp


def _const_spec(shape):
    return pl.BlockSpec(shape, lambda b, t: (0,) * len(shape))


def _mixer_call(body, rows, bsz, lp, out_w, out_dtype, consts, scratch, name, h):
    tile = pl.BlockSpec((1, rows, D_MODEL), lambda b, t: (b, t, 0))
    return pl.pallas_call(
        body,
        out_shape=jax.ShapeDtypeStruct((bsz, lp, out_w), out_dtype),
        grid=(bsz, lp // rows),
        in_specs=[tile] + [_const_spec(c.shape) for c in consts],
        out_specs=pl.BlockSpec((1, rows, out_w), lambda b, t: (b, t, 0)),
        scratch_shapes=scratch,
        compiler_params=pltpu.CompilerParams(
            dimension_semantics=("arbitrary", "arbitrary"), vmem_limit_bytes=VMEM_LIMIT),
        name=name,
    )(h, *consts)


def kernel(x, meta_tokens, norm_pre, norm_post, w_in, w_out, dn_conv, dn_A_log, dn_dt_bias, dn_norm, rw_mu, rw_w0, rw_w2, rw_a0, rw_a2, rw_k_k, rw_k_a, rw_r_k, rw_ln_w, rw_ln_b, mb_conv, mb_conv_b, mb_dt_bias, mb_A_log, mb_D, mb_norm):
    bsz, seq, _ = x.shape
    depth = w_in.shape[0]
    nsb, pad_rows, lp = _tiling(seq)
    rows = nsb * SBR
    nc = nsb * SB_CHUNKS
    meta = jnp.broadcast_to(meta_tokens.astype(x.dtype)[None], (bsz, N_META, D_MODEL))
    h = jnp.concatenate([jnp.zeros((bsz, pad_rows, D_MODEL), x.dtype), meta, x], axis=1)

    vm = lambda r, c: pltpu.VMEM((r, c), F32)
    hn_scratch = pltpu.VMEM((rows, D_MODEL), BF16)
    for l in range(depth):
        npre = norm_pre[l][None, :]
        wl = w_in[l]
        w_a = jnp.concatenate([wl[:, :4 * D_A], _pad_lanes(wl[:, 4 * D_A:4 * D_A + H_A]),
                               _pad_lanes(wl[:, 4 * D_A + H_A:P_A])], axis=1).astype(BF16)
        hp_a = jnp.stack([_pad_lanes(dn_A_log[l]), _pad_lanes(dn_dt_bias[l])])
        hp_a = jnp.pad(hp_a, ((0, 6), (0, 0)))
        o_a = _mixer_call(
            functools.partial(_deltanet_kernel, nsb, pad_rows), rows, bsz, lp, D_A, BF16,
            [npre, w_a, dn_conv[l], hp_a, dn_norm[l][None, :]],
            [hn_scratch, vm(rows + CARRY, 3 * D_A), vm(rows, D_A), vm(rows, D_A), vm(rows, D_A),
             vm(rows, D_A), vm(rows, 2 * LANES), vm(rows, LANES), vm(rows, D_A), vm(rows, D_A),
             pltpu.VMEM((H_A, DH_A, DH_A), F32)],
            "deltanet", h)
        wb_l = wl[:, P_A:P_A + P_B]
        w_b = jnp.concatenate([wb_l[:, :4 * D_B], _pad_lanes(wb_l[:, 4 * D_B:4 * D_B + R_W]),
                               _pad_lanes(wb_l[:, 4 * D_B + R_W:])], axis=1).astype(BF16)
        mu = rw_mu[l]
        mu_b = jnp.concatenate([mu[:4 * D_B], _pad_lanes(mu[4 * D_B:4 * D_B + R_W]),
                                _pad_lanes(mu[4 * D_B + R_W:])])[None, :]
        w2p = jnp.pad(rw_w2[l], ((0, LANES - R_W), (0, 0))).astype(BF16)
        a2p = jnp.pad(rw_a2[l], ((0, LANES - R_A), (0, 0))).astype(BF16)
        vp_b = jnp.stack([rw_w0[l], rw_a0[l], rw_k_k[l], rw_k_a[l], rw_r_k[l].reshape(D_B),
                          rw_ln_w[l], rw_ln_b[l], jnp.zeros((D_B,), F32)])
        o_b = _mixer_call(
            functools.partial(_rwkv_kernel, nsb, pad_rows), rows, bsz, lp, D_B, BF16,
            [npre, w_b, mu_b, w2p, a2p, vp_b],
            [hn_scratch, vm(rows + CARRY, WB)] + [vm(rows, D_B)] * 11
            + [pltpu.VMEM((H_B // 2, LANES, LANES), F32)],
            "rwkv7", h)
        wc_l = wl[:, P_A + P_B:]
        w_c = jnp.concatenate([
            _group_pad(wc_l[:, :D_C]), _group_pad(wc_l[:, D_C:2 * D_C]),
            wc_l[:, 2 * D_C:2 * D_C + 2 * G_C * N_C], _pad_lanes(wc_l[:, 2 * D_C + 2 * G_C * N_C:]),
        ], axis=1).astype(BF16)
        cw_c = jnp.concatenate([_group_pad(mb_conv[l][:, :D_C]), mb_conv[l][:, D_C:]], axis=1)
        cb_c = jnp.concatenate([_group_pad(mb_conv_b[l][:D_C]), mb_conv_b[l][D_C:]])[None, :]
        hp_c = jnp.pad(jnp.stack([_pad_lanes(mb_dt_bias[l]), _pad_lanes(mb_A_log[l])]),
                       ((0, 6), (0, 0)))
        vp_c = jnp.pad(jnp.stack([_group_pad(jnp.repeat(mb_D[l], DH_C)), _group_pad(mb_norm[l])]),
                       ((0, 6), (0, 0)))
        o_c = _mixer_call(
            functools.partial(_mamba_kernel, nc, pad_rows), rows, bsz, lp, D_CP, F32,
            [npre, w_c, cw_c, cb_c, hp_c, vp_c],
            [hn_scratch, vm(rows + CARRY, WC_CONV), vm(rows, D_CP), vm(rows, D_CP),
             vm(rows, G_C * N_C),
             vm(rows, G_C * N_C), vm(rows, LANES), vm(rows, LANES), vm(rows, D_CP),
             pltpu.VMEM((G_C, GW, N_C), F32)],
            "mamba2", h)
        wo = w_out[l]
        wo_a = wo[:D_A].astype(BF16)
        wo_b = wo[D_A:D_A + D_B].astype(BF16)
        wo_c = jnp.pad(wo[D_A + D_B:].reshape(G_C, HPG * DH_C, D_MODEL),
                       ((0, 0), (0, GW - HPG * DH_C), (0, 0))).reshape(D_CP, D_MODEL).astype(BF16)
        lead = pad_rows + N_META
        direct = l == depth - 1 and lead % SBR == 0 and seq % SBR == 0
        orows = SBR if direct else rows
        row_spec = lambda w: pl.BlockSpec((1, orows, w), lambda b, t: (b, t, 0))
        if direct:
            out_rows = seq
            out_spec = pl.BlockSpec((1, orows, D_MODEL),
                                    lambda b, t: (b, jnp.maximum(t - lead // SBR, 0), 0))
        else:
            out_rows = lp
            out_spec = row_spec(D_MODEL)
        h = pl.pallas_call(
            _out_kernel,
            out_shape=jax.ShapeDtypeStruct((bsz, out_rows, D_MODEL), F32),
            grid=(bsz, lp // orows),
            in_specs=[row_spec(D_MODEL), row_spec(D_A), row_spec(D_B), row_spec(D_CP),
                      _const_spec(wo_a.shape), _const_spec(wo_b.shape), _const_spec(wo_c.shape),
                      _const_spec((1, D_MODEL))],
            out_specs=out_spec,
            compiler_params=pltpu.CompilerParams(
                dimension_semantics=("arbitrary", "arbitrary"), vmem_limit_bytes=VMEM_LIMIT),
            name="out_proj",
        )(h, o_a, o_b, o_c, wo_a, wo_b, wo_c, norm_post[l][None, :])
    return h if h.shape[1] == seq else h[:, pad_rows + N_META:]
```
